```python
import math
import jax, jax.numpy as jnp
from jax import lax
import numpy as np

D_MODEL = 2048
BATCH = 8
SEQ = 4096
DEPTH = 4
DEC_BATCH = 8
DEC_SEQ = 16
PAST_LEN = 4096

CHUNK = 64
N_META = 16
N_A = DEPTH // 2
N_B = DEPTH - N_A
N_DENSE = (DEPTH + 1) // 2
N_MOE = DEPTH // 2
CONV_W = 31
HEAD_DIM = 64
N_HEADS = D_MODEL // HEAD_DIM
N_KV_HEADS = N_HEADS // 8
GROUP = N_HEADS // N_KV_HEADS
WINDOW = 128
WIN_CHUNKS = WINDOW // CHUNK
ROPE_THETA = 10000.0
D_FF = 5632
N_EXPERTS = 8
TOP_K = 2
D_EXPERT = 7168
MOE_BLOCK = 256
EPS = 1e-6
NEG_INF = -1e30

kernel_name = 'yoco_conformer_swa_sink_stream_step'


def _rms(x, g):
    xf = x.astype(jnp.float32)
    y = xf * lax.rsqrt(jnp.mean(xf * xf, axis=-1, keepdims=True) + EPS)
    return (y * g.astype(jnp.float32)).astype(x.dtype)


def _layernorm(x, g, b):
    xf = x.astype(jnp.float32)
    mu = jnp.mean(xf, axis=-1, keepdims=True)
    xc = xf - mu
    y = xc * lax.rsqrt(jnp.mean(xc * xc, axis=-1, keepdims=True) + EPS)
    return (y * g.astype(jnp.float32) + b.astype(jnp.float32)).astype(x.dtype)


def _rope(x, pos):
    half = HEAD_DIM // 2
    inv = jnp.exp(jnp.arange(half, dtype=jnp.float32) * (-2.0 * math.log(ROPE_THETA) / HEAD_DIM))
    ang = pos.astype(jnp.float32)[:, None] * inv[None, :]
    cos = jnp.cos(ang)[:, None, :]
    sin = jnp.sin(ang)[:, None, :]
    xf = x.astype(jnp.float32)
    x1, x2 = xf[..., :half], xf[..., half:]
    return jnp.concatenate([x1 * cos - x2 * sin, x2 * cos + x1 * sin], axis=-1).astype(x.dtype)


def _conv_module(x, conv_prev, g, w_pw1, b_pw1, w_dw, b_dw, g_ln, b_ln, w_pw2, b_pw2):
    h = _rms(x, g)
    u = h @ w_pw1 + b_pw1
    u = u[..., :D_MODEL] * jax.nn.sigmoid(u[..., D_MODEL:])
    u_ext = jnp.concatenate([conv_prev.astype(u.dtype), u], axis=1)
    c = lax.conv_general_dilated(u_ext, w_dw[:, None, :].astype(u.dtype), window_strides=(1,),
                                 padding='VALID', dimension_numbers=('NWC', 'WIO', 'NWC'),
                                 feature_group_count=D_MODEL) + b_dw
    c = jax.nn.silu(_layernorm(c, g_ln, b_ln))
    return c @ w_pw2 + b_pw2, u_ext[:, -(CONV_W - 1):]


def _swiglu(x, g, wg, wu, wd):
    h = _rms(x, g)
    return (jax.nn.silu(h @ wg) * (h @ wu)) @ wd


def _moe(x, g, w_router, w_gate, w_up, w_down):
    B, S, D = x.shape
    T = B * S
    h = _rms(x, g).reshape(T, D)
    logits = h.astype(jnp.float32) @ w_router.astype(jnp.float32)
    top_val, top_idx = lax.top_k(logits, TOP_K)
    gates = jax.nn.softmax(top_val, axis=-1)
    M = T * TOP_K
    e_flat = top_idx.reshape(M)
    t_flat = jnp.arange(M, dtype=jnp.int32) // TOP_K
    g_flat = gates.reshape(M)
    order = jnp.argsort(e_flat)
    e_s, t_s, g_s = e_flat[order], t_flat[order], g_flat[order]
    counts = jnp.bincount(e_flat, length=N_EXPERTS)
    starts = jnp.cumsum(counts) - counts
    padded = (counts + MOE_BLOCK - 1) // MOE_BLOCK * MOE_BLOCK
    pad_ends = jnp.cumsum(padded)
    pad_starts = pad_ends - padded
    dest = pad_starts[e_s] + (jnp.arange(M, dtype=jnp.int32) - starts[e_s])
    n_blocks = -(-M // MOE_BLOCK) + N_EXPERTS
    buf = jnp.zeros((n_blocks * MOE_BLOCK, D), h.dtype).at[dest].set(h[t_s])
    block_expert = jnp.minimum(
        jnp.searchsorted(pad_ends, jnp.arange(n_blocks, dtype=jnp.int32) * MOE_BLOCK, side='right'),
        N_EXPERTS - 1)

    def expert_block(args):
        xb, e = args
        return (jax.nn.silu(xb @ w_gate[e]) * (xb @ w_up[e])) @ w_down[e]

    y_buf = lax.map(expert_block, (buf.reshape(n_blocks, MOE_BLOCK, D), block_expert))
    y_buf = y_buf.reshape(n_blocks * MOE_BLOCK, D)
    rows = (g_s[:, None] * y_buf[dest].astype(jnp.float32)).astype(h.dtype)
    return jnp.zeros((T, D), h.dtype).at[t_s].add(rows).reshape(B, S, D)


def _channel_mixer(x, l, ffn, moe):
    i = l // 2
    if l % 2 == 0:
        g, wg, wu, wd = ffn
        return x + _swiglu(x, g[i], wg[i], wu[i], wd[i])
    g, wr, wg, wu, wd = moe
    return x + _moe(x, g[i], wr[i], wg[i], wu[i], wd[i])


def _shared_kv(x, pos, g_kv, w_kv, g_k):
    B, S, _ = x.shape
    kv = _rms(x, g_kv) @ w_kv
    k = kv[..., :N_KV_HEADS * HEAD_DIM].reshape(B, S, N_KV_HEADS, HEAD_DIM)
    v = kv[..., N_KV_HEADS * HEAD_DIM:].reshape(B, S, N_KV_HEADS, HEAD_DIM)
    return _rope(_rms(k, g_k), pos), v


def _queries(x, pos, g_attn, w_q, g_q):
    B, S, _ = x.shape
    q = (_rms(x, g_attn) @ w_q).reshape(B, S, N_HEADS, HEAD_DIM)
    return _rope(_rms(q, g_q), pos)


def _sink_softmax(scores, sinks):
    sk = jnp.broadcast_to(sinks.astype(jnp.float32).reshape(N_KV_HEADS, GROUP, 1, 1),
                          scores.shape[:-1] + (1,))
    p = jax.nn.softmax(jnp.concatenate([sk, scores], axis=-1), axis=-1)
    return p[..., 1:]


def _swa_prompt(q, k_meta, v_meta, k_fr, v_fr, sinks, w_o):
    B, S = q.shape[:2]
    nC = S // CHUNK
    scale = HEAD_DIM ** -0.5
    qb = q.reshape(B, nC, CHUNK, N_KV_HEADS, GROUP, HEAD_DIM)
    pad = jnp.zeros((B, WIN_CHUNKS * CHUNK, N_KV_HEADS, HEAD_DIM), k_fr.dtype)
    kp = jnp.concatenate([pad, k_fr], axis=1).reshape(B, nC + WIN_CHUNKS, CHUNK, N_KV_HEADS, HEAD_DIM)
    vp = jnp.concatenate([pad.astype(v_fr.dtype), v_fr], axis=1).reshape(B, nC + WIN_CHUNKS, CHUNK, N_KV_HEADS, HEAD_DIM)
    k_band = jnp.concatenate([kp[:, j:j + nC] for j in range(WIN_CHUNKS + 1)], axis=2)
    v_band = jnp.concatenate([vp[:, j:j + nC] for j in range(WIN_CHUNKS + 1)], axis=2)
    key_chunk = jnp.arange(nC)[:, None] + jnp.arange(WIN_CHUNKS + 1)[None, :] - WIN_CHUNKS
    valid = jnp.repeat(key_chunk >= 0, CHUNK, axis=1)
    s_meta = jnp.einsum('bnqkgd,bmkd->bnkgqm', qb, k_meta).astype(jnp.float32) * scale
    s_band = jnp.einsum('bnqkgd,bnskd->bnkgqs', qb, k_band).astype(jnp.float32) * scale
    s_band = jnp.where(valid[None, :, None, None, None, :], s_band, NEG_INF)
    p = _sink_softmax(jnp.concatenate([s_meta, s_band], axis=-1), sinks).astype(v_fr.dtype)
    o = (jnp.einsum('bnkgqm,bmkd->bnqkgd', p[..., :N_META], v_meta)
         + jnp.einsum('bnkgqs,bnskd->bnqkgd', p[..., N_META:], v_band))
    return o.reshape(B, S, N_HEADS * HEAD_DIM) @ w_o


def _swa_sample(q, k_meta, v_meta, k_win, v_win, k_new, v_new, sinks, w_o):
    B, S = q.shape[:2]
    scale = HEAD_DIM ** -0.5
    qs = q.reshape(B, S, N_KV_HEADS, GROUP, HEAD_DIM)
    k_all = jnp.concatenate([k_meta.astype(k_new.dtype), k_win.astype(k_new.dtype), k_new], axis=1)
    v_all = jnp.concatenate([v_meta.astype(v_new.dtype), v_win.astype(v_new.dtype), v_new], axis=1)
    s = jnp.einsum('bqkgd,bskd->bkgqs', qs, k_all).astype(jnp.float32) * scale
    p = _sink_softmax(s, sinks).astype(v_all.dtype)
    o = jnp.einsum('bkgqs,bskd->bqkgd', p, v_all)
    return o.reshape(B, S, N_HEADS * HEAD_DIM) @ w_o


def setup_inputs(seed: int = 0) -> dict:
    key = jax.random.key(seed)
    ks = jax.random.split(key, 34)
    f32 = jnp.float32
    D = D_MODEL
    HQ = N_HEADS * HEAD_DIM
    HKV = N_KV_HEADS * HEAD_DIM

    def nrm(i, shape, scale):
        return jax.random.normal(ks[i], shape, f32) * scale

    def gain(i, shape):
        return 1.0 + 0.05 * jax.random.normal(ks[i], shape, f32)

    return {
        'x_prompt': nrm(0, (BATCH, SEQ, D), 1.0),
        'x_sample': nrm(1, (DEC_BATCH, DEC_SEQ, D), 1.0),
        'state_conv': nrm(2, (N_A, DEC_BATCH, CONV_W - 1, D), 0.5),
        'cache_meta_k': nrm(3, (DEC_BATCH, N_META, N_KV_HEADS, HEAD_DIM), 1.0),
        'cache_meta_v': nrm(4, (DEC_BATCH, N_META, N_KV_HEADS, HEAD_DIM), 1.0),
        'cache_win_k': nrm(5, (DEC_BATCH, WINDOW, N_KV_HEADS, HEAD_DIM), 1.0),
        'cache_win_v': nrm(6, (DEC_BATCH, WINDOW, N_KV_HEADS, HEAD_DIM), 1.0),
        'meta_tokens': nrm(7, (N_META, D), 1.0),
        'g_conv': gain(8, (N_A, D)),
        'w_pw1': nrm(9, (N_A, D, 2 * D), D ** -0.5),
        'b_pw1': nrm(10, (N_A, 2 * D), 0.02),
        'w_dw': nrm(11, (N_A, CONV_W, D), CONV_W ** -0.5),
        'b_dw': nrm(12, (N_A, D), 0.02),
        'g_conv_ln': gain(13, (N_A, D)),
        'b_conv_ln': nrm(14, (N_A, D), 0.02),
        'w_pw2': nrm(15, (N_A, D, D), D ** -0.5),
        'b_pw2': nrm(16, (N_A, D), 0.02),
        'g_kv': gain(17, (D,)),
        'w_kv': nrm(18, (D, 2 * HKV), D ** -0.5),
        'g_k': gain(19, (HEAD_DIM,)),
        'g_attn': gain(20, (N_B, D)),
        'w_q': nrm(21, (N_B, D, HQ), D ** -0.5),
        'g_q': gain(22, (N_B, HEAD_DIM)),
        'sinks': nrm(23, (N_B, N_HEADS), 0.5),
        'w_o': nrm(24, (N_B, HQ, D), HQ ** -0.5),
        'g_ffn': gain(25, (N_DENSE, D)),
        'w_ffn_gate': nrm(26, (N_DENSE, D, D_FF), D ** -0.5),
        'w_ffn_up': nrm(27, (N_DENSE, D, D_FF), D ** -0.5),
        'w_ffn_down': nrm(28, (N_DENSE, D_FF, D), D_FF ** -0.5),
        'g_moe': gain(29, (N_MOE, D)),
        'w_router': nrm(30, (N_MOE, D, N_EXPERTS), D ** -0.5),
        'w_moe_gate': nrm(31, (N_MOE, N_EXPERTS, D, D_EXPERT), D ** -0.5),
        'w_moe_up': nrm(32, (N_MOE, N_EXPERTS, D, D_EXPERT), D ** -0.5),
        'w_moe_down': nrm(33, (N_MOE, N_EXPERTS, D_EXPERT, D), D_EXPERT ** -0.5),
    }


def reference(x_prompt, x_sample, state_conv, cache_meta_k, cache_meta_v, cache_win_k, cache_win_v,
              meta_tokens, g_conv, w_pw1, b_pw1, w_dw, b_dw, g_conv_ln, b_conv_ln, w_pw2, b_pw2,
              g_kv, w_kv, g_k, g_attn, w_q, g_q, sinks, w_o,
              g_ffn, w_ffn_gate, w_ffn_up, w_ffn_down,
              g_moe, w_router, w_moe_gate, w_moe_up, w_moe_down):
    conv = (g_conv, w_pw1, b_pw1, w_dw, b_dw, g_conv_ln, b_conv_ln, w_pw2, b_pw2)
    ffn = (g_ffn, w_ffn_gate, w_ffn_up, w_ffn_down)
    moe = (g_moe, w_router, w_moe_gate, w_moe_up, w_moe_down)

    B = x_prompt.shape[0]
    S = x_prompt.shape[1]
    x = jnp.concatenate([jnp.broadcast_to(meta_tokens.astype(x_prompt.dtype)[None], (B, N_META, D_MODEL)),
                         x_prompt], axis=1)
    conv_p = []
    pos_fr = N_META + jnp.arange(S)
    for l in range(DEPTH):
        if l < N_A:
            prev = jnp.zeros((B, CONV_W - 1, D_MODEL), x.dtype)
            d, st = _conv_module(x, prev, *[t[l] for t in conv])
            x = x + d
            conv_p.append(st)
        else:
            if l == N_A:
                k, v = _shared_kv(x, jnp.arange(x.shape[1]), g_kv, w_kv, g_k)
                k_meta_p, v_meta_p = k[:, :N_META], v[:, :N_META]
                k_fr, v_fr = k[:, N_META:], v[:, N_META:]
                x = x[:, N_META:]
            j = l - N_A
            q = _queries(x, pos_fr, g_attn[j], w_q[j], g_q[j])
            x = x + _swa_prompt(q, k_meta_p, v_meta_p, k_fr, v_fr, sinks[j], w_o[j])
        x = _channel_mixer(x, l, ffn, moe)
    y_prompt = x
    win_k_p = k_fr[:, -WINDOW:]
    win_v_p = v_fr[:, -WINDOW:]

    xs = x_sample
    conv_s = []
    pos_new = N_META + PAST_LEN + jnp.arange(xs.shape[1])
    for l in range(DEPTH):
        if l < N_A:
            d, st = _conv_module(xs, state_conv[l], *[t[l] for t in conv])
            xs = xs + d
            conv_s.append(st)
        else:
            if l == N_A:
                k_new, v_new = _shared_kv(xs, pos_new, g_kv, w_kv, g_k)
            j = l - N_A
            q = _queries(xs, pos_new, g_attn[j], w_q[j], g_q[j])
            xs = xs + _swa_sample(q, cache_meta_k, cache_meta_v, cache_win_k, cache_win_v,
                                  k_new, v_new, sinks[j], w_o[j])
        xs = _channel_mixer(xs, l, ffn, moe)
    y_sample = xs
    win_k_s = jnp.concatenate([cache_win_k.astype(k_new.dtype), k_new], axis=1)[:, -WINDOW:]
    win_v_s = jnp.concatenate([cache_win_v.astype(v_new.dtype), v_new], axis=1)[:, -WINDOW:]
    conv_state_prompt = jnp.stack(conv_p)
    conv_state_sample = jnp.stack(conv_s)
    return (y_prompt, y_sample, conv_state_prompt, conv_state_sample, k_meta_p, v_meta_p,
            win_k_p, win_v_p, win_k_s, win_v_s)
```

```python
import functools
import math

import jax
import jax.numpy as jnp
from jax import lax
from jax.experimental import pallas as pl
from jax.experimental.pallas import tpu as pltpu

D_MODEL = 2048
BATCH = 8
SEQ = 4096
DEPTH = 4
DEC_BATCH = 8
DEC_SEQ = 16
PAST_LEN = 4096
CHUNK = 64
N_META = 16
N_A = DEPTH // 2
CONV_W = 31
HEAD_DIM = 64
N_HEADS = D_MODEL // HEAD_DIM
N_KV_HEADS = N_HEADS // 8
GROUP = N_HEADS // N_KV_HEADS
WINDOW = 128
WIN_CHUNKS = WINDOW // CHUNK
ROPE_THETA = 10000.0
D_FF = 5632
N_EXPERTS = 8
TOP_K = 2
D_EXPERT = 7168
EPS = 1e-6
NEG_INF = -1e30

F32 = jnp.float32
BF16 = jnp.bfloat16

N_PROMPT = BATCH * SEQ
N_SAMPLE = DEC_BATCH * DEC_SEQ
N_SMALL = 256
M_ROWS = N_PROMPT + N_SMALL
ROW_SAMPLE = N_PROMPT
ROW_META = N_PROMPT + N_SAMPLE

TM = 768
TN = 1024
TF = 512
TB = TM
N_ASSIGN = M_ROWS * TOP_K
N_BLOCKS = N_ASSIGN // TB + N_EXPERTS
TR = 384
SLAB = 256
HIST = 32
CONV_TT = 128
CONV_RG = 32
CONV_LS = 512
KV_W = N_KV_HEADS * HEAD_DIM
KEYS = 256
BAND = (WIN_CHUNKS + 1) * CHUNK
ATT_QB = 8
VMEM_LIMIT = 56 * 1024 * 1024


def _cparams(sem):
    return pltpu.CompilerParams(dimension_semantics=sem, vmem_limit_bytes=VMEM_LIMIT)


def _dot(a, b):
    return jnp.dot(a, b, preferred_element_type=F32)


def _norm_kernel(*refs, n_y, gated, n_g, write_x):
    x_ref = refs[0]
    y_refs = refs[1:1 + n_y]
    pos = 1 + n_y
    gate_ref = refs[pos] if gated else None
    pos += 1 if gated else 0
    g_refs = refs[pos:pos + n_g]
    pos += n_g
    outs = refs[pos:]
    x = x_ref[...]
    for i, y_ref in enumerate(y_refs):
        y = y_ref[...]
        x = x + (gate_ref[:, i:i + 1] * y if gated else y)
    o = 0
    if write_x:
        outs[0][...] = x
        o = 1
    if n_g:
        xn = x * lax.rsqrt(jnp.mean(x * x, axis=-1, keepdims=True) + EPS)
        for i, g_ref in enumerate(g_refs):
            outs[o + i][...] = (xn * g_ref[...]).astype(BF16)


def _norm(x, ys=(), gates=None, gains=()):
    m = x.shape[0]
    n_y, n_g = len(ys), len(gains)
    write_x = n_y > 0
    row = pl.BlockSpec((TR, D_MODEL), lambda i: (i, 0))
    in_specs = [row] * (1 + n_y)
    args = [x, *ys]
    if gates is not None:
        in_specs.append(pl.BlockSpec((TR, 128), lambda i: (i, 0)))
        args.append(gates)
    for g in gains:
        in_specs.append(pl.BlockSpec((1, D_MODEL), lambda i: (0, 0)))
        args.append(g.reshape(1, D_MODEL))
    out_shape, out_specs = [], []
    if write_x:
        out_shape.append(jax.ShapeDtypeStruct((m, D_MODEL), F32))
        out_specs.append(row)
    for _ in gains:
        out_shape.append(jax.ShapeDtypeStruct((m, D_MODEL), BF16))
        out_specs.append(row)
    return pl.pallas_call(
        functools.partial(_norm_kernel, n_y=n_y, gated=gates is not None, n_g=n_g, write_x=write_x),
        grid=(m // TR,), in_specs=in_specs, out_specs=out_specs, out_shape=out_shape,
        compiler_params=_cparams(("parallel",)), name="norm")(*args)


def _glu_kernel(h_ref, wa_ref, wb_ref, ba_ref, bb_ref, u_ref):
    h = h_ref[...]
    a = _dot(h, wa_ref[...]) + ba_ref[...]
    b = _dot(h, wb_ref[...]) + bb_ref[...]
    u_ref[...] = a * jax.nn.sigmoid(b)


def _glu(h, w, b):
    nj = D_MODEL // TN
    b2 = b.reshape(1, 2 * D_MODEL)
    return pl.pallas_call(
        _glu_kernel, grid=(M_ROWS // TM, nj),
        in_specs=[pl.BlockSpec((TM, D_MODEL), lambda i, j: (i, 0)),
                  pl.BlockSpec((D_MODEL, TN), lambda i, j: (0, j)),
                  pl.BlockSpec((D_MODEL, TN), lambda i, j: (0, j + nj)),
                  pl.BlockSpec((1, TN), lambda i, j: (0, j)),
                  pl.BlockSpec((1, TN), lambda i, j: (0, j + nj))],
        out_specs=pl.BlockSpec((TM, TN), lambda i, j: (i, j)),
        out_shape=jax.ShapeDtypeStruct((M_ROWS, D_MODEL), F32),
        compiler_params=_cparams(("parallel", "parallel")), name="pw1_glu")(h, w, w, b2, b2)


def _resid_kernel(x_ref, w_ref, b_ref, r_ref, o_ref):
    o_ref[...] = r_ref[...] + (_dot(x_ref[...], w_ref[...]) + b_ref[...])


def _resid_mm(x, w, b, r):
    return pl.pallas_call(
        _resid_kernel, grid=(M_ROWS // TM, D_MODEL // TN),
        in_specs=[pl.BlockSpec((TM, D_MODEL), lambda i, j: (i, 0)),
                  pl.BlockSpec((D_MODEL, TN), lambda i, j: (0, j)),
                  pl.BlockSpec((1, TN), lambda i, j: (0, j)),
                  pl.BlockSpec((TM, TN), lambda i, j: (i, j))],
        out_specs=pl.BlockSpec((TM, TN), lambda i, j: (i, j)),
        out_shape=jax.ShapeDtypeStruct((M_ROWS, D_MODEL), F32),
        compiler_params=_cparams(("parallel", "parallel")), name="proj_resid")(x, w, b.reshape(1, D_MODEL), r)


def _head_norm_rope(x, gain, cos, sin, bd):
    ssq = _dot((x * x).astype(BF16), bd)
    xn = x * lax.rsqrt(ssq * (1.0 / HEAD_DIM) + EPS) * gain
    lane = lax.broadcasted_iota(jnp.int32, xn.shape, 1)
    first_half = (lane % HEAD_DIM) < (HEAD_DIM // 2)
    rot = jnp.where(first_half, pltpu.roll(xn, SLAB - HEAD_DIM // 2, 1), pltpu.roll(xn, HEAD_DIM // 2, 1))
    return xn * cos + rot * sin


def _q_kernel(h_ref, w_ref, g_ref, cos_ref, sin_ref, bd_ref, q_ref):
    acc = _dot(h_ref[...], w_ref[...])
    cos, sin, bd, gain = cos_ref[...], sin_ref[...], bd_ref[...], g_ref[...]
    scale = HEAD_DIM ** -0.5
    for s in range(TN // SLAB):
        q = _head_norm_rope(acc[:, s * SLAB:(s + 1) * SLAB], gain, cos, sin, bd)
        q_ref[:, s * SLAB:(s + 1) * SLAB] = (q * scale).astype(BF16)


def _queries(h, w, g_q, cos, sin, bd):
    return pl.pallas_call(
        _q_kernel, grid=(M_ROWS // TM, D_MODEL // TN),
        in_specs=[pl.BlockSpec((TM, D_MODEL), lambda i, j: (i, 0)),
                  pl.BlockSpec((D_MODEL, TN), lambda i, j: (0, j)),
                  pl.BlockSpec((1, SLAB), lambda i, j: (0, 0)),
                  pl.BlockSpec((TM, SLAB), lambda i, j: (i, 0)),
                  pl.BlockSpec((TM, SLAB), lambda i, j: (i, 0)),
                  pl.BlockSpec((SLAB, SLAB), lambda i, j: (0, 0))],
        out_specs=pl.BlockSpec((TM, TN), lambda i, j: (i, j)),
        out_shape=jax.ShapeDtypeStruct((M_ROWS, D_MODEL), BF16),
        compiler_params=_cparams(("parallel", "parallel")), name="q_proj")(
            h, w, jnp.tile(g_q, SLAB // HEAD_DIM).reshape(1, SLAB), cos, sin, bd)


def _kv_kernel(h_ref, w_ref, g_ref, cos_ref, sin_ref, bd_ref, k_ref, v_ref):
    acc = _dot(h_ref[...], w_ref[...])
    k_ref[...] = _head_norm_rope(acc[:, :KV_W], g_ref[...], cos_ref[...], sin_ref[...], bd_ref[...])
    v_ref[...] = acc[:, KV_W:]


def _shared_kv(h, w, g_k, cos, sin, bd):
    return pl.pallas_call(
        _kv_kernel, grid=(M_ROWS // TM,),
        in_specs=[pl.BlockSpec((TM, D_MODEL), lambda i: (i, 0)),
                  pl.BlockSpec((D_MODEL, 2 * KV_W), lambda i: (0, 0)),
                  pl.BlockSpec((1, SLAB), lambda i: (0, 0)),
                  pl.BlockSpec((TM, SLAB), lambda i: (i, 0)),
                  pl.BlockSpec((TM, SLAB), lambda i: (i, 0)),
                  pl.BlockSpec((SLAB, SLAB), lambda i: (0, 0))],
        out_specs=[pl.BlockSpec((TM, KV_W), lambda i: (i, 0))] * 2,
        out_shape=[jax.ShapeDtypeStruct((M_ROWS, KV_W), F32)] * 2,
        compiler_params=_cparams(("parallel",)), name="kv_proj")(
            h, w, jnp.tile(g_k, SLAB // HEAD_DIM).reshape(1, SLAB), cos, sin, bd)


def _conv_kernel(*refs, tt, use_prev):
    if use_prev:
        u_ref, prev_ref, hist_ref, w_ref, bdw_ref, gln_ref, bln_ref, c_ref, ext_ref, acc_ref = refs
    else:
        u_ref, hist_ref, w_ref, bdw_ref, gln_ref, bln_ref, c_ref, ext_ref, acc_ref = refs
    if use_prev:
        first = pl.program_id(1) == 0

        @pl.when(first)
        def _():
            ext_ref[0:HIST, :] = hist_ref[0]

        @pl.when(jnp.logical_not(first))
        def _():
            ext_ref[0:HIST, :] = prev_ref[...]
    else:
        ext_ref[0:HIST, :] = hist_ref[0]
    ext_ref[HIST:HIST + tt, :] = u_ref[...]
    rg = min(CONV_RG, tt)
    off = HIST - (CONV_W - 1)
    for r in range(tt // rg):
        for s in range(D_MODEL // CONV_LS):
            ls = slice(s * CONV_LS, (s + 1) * CONV_LS)
            acc = jnp.zeros((rg, CONV_LS), F32)
            for k in range(CONV_W):
                acc = acc + ext_ref[r * rg + off + k:r * rg + off + k + rg, ls] * w_ref[k:k + 1, ls]
            acc_ref[r * rg:(r + 1) * rg, ls] = acc
    x = acc_ref[...] + bdw_ref[...]
    xc = x - jnp.mean(x, axis=-1, keepdims=True)
    y = xc * lax.rsqrt(jnp.mean(xc * xc, axis=-1, keepdims=True) + EPS)
    y = y * gln_ref[...] + bln_ref[...]
    c_ref[...] = (y * jax.nn.sigmoid(y)).astype(BF16)


def _conv(u, hist, w_dw, b_dw, g_ln, b_ln, *, row0, n_seq, seq_len, tt):
    n_t = seq_len // tt
    use_prev = n_t > 1
    blk0 = row0 // tt
    wpad = jnp.zeros((32, D_MODEL), F32).at[:CONV_W].set(w_dw)
    vec = pl.BlockSpec((1, D_MODEL), lambda s, i: (0, 0))
    in_specs = [pl.BlockSpec((tt, D_MODEL), lambda s, i: (blk0 + s * n_t + i, 0))]
    args = [u]
    if use_prev:
        per = tt // HIST
        in_specs.append(pl.BlockSpec(
            (HIST, D_MODEL), lambda s, i: (jnp.maximum((blk0 + s * n_t + i) * per - 1, 0), 0)))
        args.append(u)
    in_specs += [pl.BlockSpec((1, HIST, D_MODEL), lambda s, i: (s, 0, 0)),
                 pl.BlockSpec((32, D_MODEL), lambda s, i: (0, 0)), vec, vec, vec]
    args += [hist, wpad, b_dw.reshape(1, D_MODEL), g_ln.reshape(1, D_MODEL), b_ln.reshape(1, D_MODEL)]
    return pl.pallas_call(
        functools.partial(_conv_kernel, tt=tt, use_prev=use_prev), grid=(n_seq, n_t),
        in_specs=in_specs,
        out_specs=pl.BlockSpec((tt, D_MODEL), lambda s, i: (s * n_t + i, 0)),
        out_shape=jax.ShapeDtypeStruct((n_seq * seq_len, D_MODEL), BF16),
        scratch_shapes=[pltpu.VMEM((HIST + tt, D_MODEL), F32), pltpu.VMEM((tt, D_MODEL), F32)],
        compiler_params=_cparams(("parallel", "arbitrary")), name="conv_ln_swish")(*args)


def _ffn_kernel(be_ref, ok_ref, x_ref, wg_ref, wu_ref, wd_ref, o_ref, *, n_f):
    i, f = pl.program_id(0), pl.program_id(1)
    ok = ok_ref[i] == 1

    @pl.when(ok)
    def _():
        x = x_ref[...]
        g = _dot(x, wg_ref[0])
        u = _dot(x, wu_ref[0])
        a = (g * jax.nn.sigmoid(g) * u).astype(BF16)
        part = _dot(a, wd_ref[0])

        @pl.when(f == 0)
        def _():
            o_ref[...] = part

        @pl.when(f != 0)
        def _():
            o_ref[...] += part

    @pl.when(jnp.logical_and(jnp.logical_not(ok), f == 0))
    def _():
        o_ref[...] = jnp.zeros_like(o_ref)


def _ffn(x, block_expert, block_ok, wg, wu, wd):
    n_blocks = x.shape[0] // TB
    d_hidden = wg.shape[-1]
    n_f = d_hidden // TF

    def f_idx(i, f, ok):
        return jnp.where(ok[i] == 1, f, n_f - 1)

    grid_spec = pltpu.PrefetchScalarGridSpec(
        num_scalar_prefetch=2, grid=(n_blocks, n_f),
        in_specs=[pl.BlockSpec((TB, D_MODEL), lambda i, f, be, ok: (i, 0)),
                  pl.BlockSpec((1, D_MODEL, TF), lambda i, f, be, ok: (be[i], 0, f_idx(i, f, ok))),
                  pl.BlockSpec((1, D_MODEL, TF), lambda i, f, be, ok: (be[i], 0, f_idx(i, f, ok))),
                  pl.BlockSpec((1, TF, D_MODEL), lambda i, f, be, ok: (be[i], f_idx(i, f, ok), 0))],
        out_specs=pl.BlockSpec((TB, D_MODEL), lambda i, f, be, ok: (i, 0)))
    return pl.pallas_call(
        functools.partial(_ffn_kernel, n_f=n_f), grid_spec=grid_spec,
        out_shape=jax.ShapeDtypeStruct((n_blocks * TB, D_MODEL), F32),
        compiler_params=_cparams(("parallel", "arbitrary")), name="swiglu")(
            block_expert, block_ok, x, wg, wu, wd)


def _router_kernel(x_ref, g_ref, wr_ref, h_ref, idx_ref, gate_ref):
    x = x_ref[...]
    h = x * lax.rsqrt(jnp.mean(x * x, axis=-1, keepdims=True) + EPS) * g_ref[...]
    h_ref[...] = h.astype(BF16)
    logits = jnp.dot(h, wr_ref[...], preferred_element_type=F32, precision=lax.Precision.HIGHEST)
    lane_i = lax.broadcasted_iota(jnp.int32, logits.shape, 1)
    lane = lane_i.astype(F32)
    logits = jnp.where(lane_i < N_EXPERTS, logits, -jnp.inf)
    m1 = jnp.max(logits, axis=-1, keepdims=True)
    i1 = jnp.min(jnp.where(logits == m1, lane, 128.0), axis=-1, keepdims=True)
    rest = jnp.where(lane == i1, -jnp.inf, logits)
    m2 = jnp.max(rest, axis=-1, keepdims=True)
    i2 = jnp.min(jnp.where(rest == m2, lane, 128.0), axis=-1, keepdims=True)
    e2 = jnp.exp(m2 - m1)
    g1 = 1.0 / (1.0 + e2)
    idx_ref[...] = jnp.where(lane_i == 0, i1, i2).astype(jnp.int32)
    gate_ref[...] = jnp.where(lane_i == 0, g1, e2 * g1)


def _router(x, g, w_router):
    wr = jnp.zeros((D_MODEL, 128), F32).at[:, :N_EXPERTS].set(w_router)
    return pl.pallas_call(
        _router_kernel, grid=(M_ROWS // TR,),
        in_specs=[pl.BlockSpec((TR, D_MODEL), lambda i: (i, 0)),
                  pl.BlockSpec((1, D_MODEL), lambda i: (0, 0)),
                  pl.BlockSpec((D_MODEL, 128), lambda i: (0, 0))],
        out_specs=[pl.BlockSpec((TR, D_MODEL), lambda i: (i, 0)),
                   pl.BlockSpec((TR, 128), lambda i: (i, 0)),
                   pl.BlockSpec((TR, 128), lambda i: (i, 0))],
        out_shape=[jax.ShapeDtypeStruct((M_ROWS, D_MODEL), BF16),
                   jax.ShapeDtypeStruct((M_ROWS, 128), jnp.int32),
                   jax.ShapeDtypeStruct((M_ROWS, 128), F32)],
        compiler_params=_cparams(("parallel",)), name="router")(x, g.reshape(1, D_MODEL), wr)


def _moe(x, g, w_router, wg, wu, wd):
    h, idx, gates = _router(x, g, w_router)
    e_flat = idx[:, :TOP_K].reshape(N_ASSIGN)
    onehot = (e_flat[:, None] == jnp.arange(N_EXPERTS, dtype=jnp.int32)[None, :]).astype(jnp.int32)
    csum = jnp.cumsum(onehot, axis=0)
    rank = jnp.sum(csum * onehot, axis=-1) - 1
    counts = csum[-1]
    padded = (counts + TB - 1) // TB * TB
    pad_ends = jnp.cumsum(padded)
    pad_starts = pad_ends - padded
    dest = pad_starts[e_flat] + rank
    blk_start = jnp.arange(N_BLOCKS, dtype=jnp.int32) * TB
    block_expert = jnp.minimum(jnp.searchsorted(pad_ends, blk_start, side='right'),
                               N_EXPERTS - 1).astype(jnp.int32)
    block_ok = (blk_start < pad_ends[-1]).astype(jnp.int32)
    src = jnp.zeros((N_BLOCKS * TB,), jnp.int32).at[dest].set(jnp.arange(N_ASSIGN, dtype=jnp.int32) // TOP_K)
    y_buf = _ffn(h[src], block_expert, block_ok, wg, wu, wd)
    d2 = dest.reshape(M_ROWS, TOP_K)
    return y_buf[d2[:, 0]], y_buf[d2[:, 1]], gates


def _attn_kernel(q_ref, k_ref, v_ref, km_ref, vm_ref, sink_ref, o_ref, *, tq, qb, hi, prompt):
    lane = lax.broadcasted_iota(jnp.int32, (KEYS, 128), 1)
    low = lane < HEAD_DIM
    key = lax.broadcasted_iota(jnp.int32, (1, 2 * KEYS), 1) % KEYS
    is_meta = jnp.logical_and(key >= BAND, key < BAND + N_META)
    zpad_k = jnp.zeros((KEYS - BAND - N_META, 128), BF16)

    def chunk(cc, carry):
        c_glob = pl.program_id(1) * qb + cc
        row0 = pl.multiple_of(cc * tq, tq)
        key0 = pl.multiple_of(c_glob * CHUNK, CHUNK) if prompt else 0
        lo = jnp.maximum(WIN_CHUNKS - c_glob, 0) * CHUNK if prompt else 0
        ok = jnp.logical_or(jnp.logical_and(key >= lo, key < hi), is_meta)
        bias = jnp.where(ok, 0.0, NEG_INF)
        for g in range(N_KV_HEADS):
            gl = slice(g * 128, (g + 1) * 128)
            k_all = jnp.concatenate([k_ref[0, pl.ds(key0, BAND), gl], km_ref[0, :, gl], zpad_k], axis=0)
            v_all = jnp.concatenate([v_ref[0, pl.ds(key0, BAND), gl], vm_ref[0, :, gl], zpad_k], axis=0)
            k2 = jnp.concatenate([jnp.where(low, k_all, 0), jnp.where(low, 0, k_all)], axis=0)
            v2 = jnp.concatenate([jnp.where(low, v_all, 0), jnp.where(low, 0, v_all)], axis=0)
            qp = jnp.concatenate(
                [q_ref[pl.ds(row0, tq), g * 512 + i * 128:g * 512 + (i + 1) * 128] for i in range(4)], axis=0)
            s = lax.dot_general(qp, k2, (((1,), (1,)), ((), ())), preferred_element_type=F32) + bias
            es, rdens = [], []
            for p in range(2):
                sp = s[:, p * KEYS:(p + 1) * KEYS]
                sk = sink_ref[g, :, p:p + 1]
                m = jnp.maximum(jnp.max(sp, axis=-1, keepdims=True), sk)
                e = jnp.exp(sp - m)
                rdens.append(1.0 / (jnp.sum(e, axis=-1, keepdims=True) + jnp.exp(sk - m)))
                es.append(e)
            pmat = jnp.concatenate([(es[0] * rdens[0]).astype(BF16), (es[1] * rdens[1]).astype(BF16)], axis=1)
            o = _dot(pmat, v2)
            for i in range(4):
                o_ref[pl.ds(row0, tq), g * 512 + i * 128:g * 512 + (i + 1) * 128] = (
                    o[i * tq:(i + 1) * tq, :].astype(BF16))
        return carry

    lax.fori_loop(0, qb, chunk, 0)


def _attention(q, kd, vd, kmd, vmd, sinks, *, row0, n_b, n_rows, tq, qb, hi, prompt):
    blk_rows = tq * qb
    n_i = n_rows // blk_rows
    blk0 = row0 // blk_rows
    per_b_meta = kmd.shape[0] > 1
    sk = sinks.reshape(N_KV_HEADS, 4, 1, 2)
    sk = jnp.broadcast_to(sk, (N_KV_HEADS, 4, tq, 2)).reshape(N_KV_HEADS, 4 * tq, 2)
    n_keys = kd.shape[1]
    kv_spec = pl.BlockSpec((1, n_keys, 512), lambda b, i: (b, 0, 0))
    meta_spec = pl.BlockSpec((1, N_META, 512), lambda b, i: (b if per_b_meta else 0, 0, 0))
    return pl.pallas_call(
        functools.partial(_attn_kernel, tq=tq, qb=qb, hi=hi, prompt=prompt), grid=(n_b, n_i),
        in_specs=[pl.BlockSpec((blk_rows, D_MODEL), lambda b, i: (blk0 + b * n_i + i, 0)),
                  kv_spec, kv_spec, meta_spec, meta_spec,
                  pl.BlockSpec((N_KV_HEADS, 4 * tq, 2), lambda b, i: (0, 0, 0))],
        out_specs=pl.BlockSpec((blk_rows, D_MODEL), lambda b, i: (b * n_i + i, 0)),
        out_shape=jax.ShapeDtypeStruct((n_b * n_rows, D_MODEL), BF16),
        compiler_params=_cparams(("parallel", "arbitrary")), name="swa")(q, kd, vd, kmd, vmd, sk)


def _dup_heads(x):
    lead = x.shape[:-1]
    x = x.astype(BF16).reshape(*lead, N_KV_HEADS, 1, HEAD_DIM)
    return jnp.broadcast_to(x, (*lead, N_KV_HEADS, 2, HEAD_DIM)).reshape(*lead, 2 * KV_W)


def kernel(x_prompt, x_sample, state_conv, cache_meta_k, cache_meta_v, cache_win_k, cache_win_v, meta_tokens, g_conv, w_pw1, b_pw1, w_dw, b_dw, g_conv_ln, b_conv_ln, w_pw2, b_pw2, g_kv, w_kv, g_k, g_attn, w_q, g_q, sinks, w_o, g_ffn, w_ffn_gate, w_ffn_up, w_ffn_down, g_moe, w_router, w_moe_gate, w_moe_up, w_moe_down):
    bf = lambda w: w.astype(BF16)
    x = jnp.concatenate([x_prompt.reshape(N_PROMPT, D_MODEL), x_sample.reshape(N_SAMPLE, D_MODEL),
                         meta_tokens, jnp.zeros((M_ROWS - ROW_META - N_META, D_MODEL), F32)], axis=0)

    pos = jnp.concatenate([jnp.tile(N_META + jnp.arange(SEQ), BATCH),
                           jnp.tile(N_META + PAST_LEN + jnp.arange(DEC_SEQ), DEC_BATCH),
                           jnp.arange(N_META), jnp.zeros((M_ROWS - ROW_META - N_META,), jnp.int32)])
    half = HEAD_DIM // 2
    inv = jnp.exp(jnp.arange(half, dtype=F32) * (-2.0 * math.log(ROPE_THETA) / HEAD_DIM))
    ang = pos.astype(F32)[:, None] * inv[None, :]
    cos_t = jnp.tile(jnp.concatenate([jnp.cos(ang), jnp.cos(ang)], axis=-1), (1, SLAB // HEAD_DIM))
    sin_t = jnp.tile(jnp.concatenate([-jnp.sin(ang), jnp.sin(ang)], axis=-1), (1, SLAB // HEAD_DIM))
    hd = jnp.arange(SLAB) // HEAD_DIM
    bd = (hd[:, None] == hd[None, :]).astype(BF16)

    dense_expert = jnp.zeros((M_ROWS // TB,), jnp.int32)
    dense_ok = jnp.ones((M_ROWS // TB,), jnp.int32)
    conv_p, conv_s = [], []

    (h,) = _norm(x, gains=(g_conv[0],))
    for l in range(N_A):
        u = _glu(h, bf(w_pw1[l]), b_pw1[l])
        u_meta = u[ROW_META:ROW_META + N_META]
        u_samp = u[ROW_SAMPLE:ROW_SAMPLE + N_SAMPLE].reshape(DEC_BATCH, DEC_SEQ, D_MODEL)
        hist_p = jnp.broadcast_to(
            jnp.concatenate([jnp.zeros((HIST - N_META, D_MODEL), F32), u_meta], axis=0)[None],
            (BATCH, HIST, D_MODEL))
        hist_s = jnp.zeros((N_SMALL // DEC_SEQ, HIST, D_MODEL), F32).at[:DEC_BATCH, HIST - (CONV_W - 1):].set(
            state_conv[l])
        conv_args = (w_dw[l], b_dw[l], g_conv_ln[l], b_conv_ln[l])
        c = jnp.concatenate([
            _conv(u, hist_p, *conv_args, row0=0, n_seq=BATCH, seq_len=SEQ, tt=CONV_TT),
            _conv(u, hist_s, *conv_args, row0=ROW_SAMPLE, n_seq=N_SMALL // DEC_SEQ, seq_len=DEC_SEQ,
                  tt=DEC_SEQ)], axis=0)
        x = _resid_mm(c, bf(w_pw2[l]), b_pw2[l], x)
        conv_p.append(u[:N_PROMPT].reshape(BATCH, SEQ, D_MODEL)[:, -(CONV_W - 1):])
        conv_s.append(jnp.concatenate([state_conv[l], u_samp], axis=1)[:, -(CONV_W - 1):])
        if l == 0:
            (h,) = _norm(x, gains=(g_ffn[0],))
            y = _ffn(h, dense_expert, dense_ok, bf(w_ffn_gate[0])[None], bf(w_ffn_up[0])[None],
                     bf(w_ffn_down[0])[None])
            x, h = _norm(x, ys=(y,), gains=(g_conv[1],))
        else:
            y0, y1, gates = _moe(x, g_moe[0], w_router[0], bf(w_moe_gate[0]), bf(w_moe_up[0]), bf(w_moe_down[0]))
            x, h_kv, h = _norm(x, ys=(y0, y1), gates=gates, gains=(g_kv, g_attn[0]))

    k, v = _shared_kv(h_kv, bf(w_kv), g_k, cos_t, sin_t, bd)
    k_fr = k[:N_PROMPT].reshape(BATCH, SEQ, KV_W)
    v_fr = v[:N_PROMPT].reshape(BATCH, SEQ, KV_W)
    k_new = k[ROW_SAMPLE:ROW_SAMPLE + N_SAMPLE].reshape(DEC_BATCH, DEC_SEQ, KV_W)
    v_new = v[ROW_SAMPLE:ROW_SAMPLE + N_SAMPLE].reshape(DEC_BATCH, DEC_SEQ, KV_W)
    k_meta_p = k[ROW_META:ROW_META + N_META]
    v_meta_p = v[ROW_META:ROW_META + N_META]
    front = jnp.zeros((BATCH, WINDOW, KV_W), F32)
    kd_p = _dup_heads(jnp.concatenate([front, k_fr], axis=1))
    vd_p = _dup_heads(jnp.concatenate([front, v_fr], axis=1))
    tail = jnp.zeros((DEC_BATCH, BAND - WINDOW - DEC_SEQ, KV_W), F32)
    kd_s = _dup_heads(jnp.concatenate([cache_win_k.reshape(DEC_BATCH, WINDOW, KV_W), k_new, tail], axis=1))
    vd_s = _dup_heads(jnp.concatenate([cache_win_v.reshape(DEC_BATCH, WINDOW, KV_W), v_new, tail], axis=1))
    kmd_p, vmd_p = _dup_heads(k_meta_p)[None], _dup_heads(v_meta_p)[None]
    kmd_s = _dup_heads(cache_meta_k.reshape(DEC_BATCH, N_META, KV_W))
    vmd_s = _dup_heads(cache_meta_v.reshape(DEC_BATCH, N_META, KV_W))
    o_rest = jnp.zeros((M_ROWS - ROW_META, D_MODEL), BF16)

    for j in range(DEPTH - N_A):
        q = _queries(h, bf(w_q[j]), g_q[j], cos_t, sin_t, bd)
        o = jnp.concatenate([
            _attention(q, kd_p, vd_p, kmd_p, vmd_p, sinks[j], row0=0, n_b=BATCH, n_rows=SEQ, tq=CHUNK,
                       qb=ATT_QB, hi=BAND, prompt=True),
            _attention(q, kd_s, vd_s, kmd_s, vmd_s, sinks[j], row0=ROW_SAMPLE, n_b=DEC_BATCH,
                       n_rows=DEC_SEQ, tq=DEC_SEQ, qb=1, hi=WINDOW + DEC_SEQ, prompt=False),
            o_rest], axis=0)
        x = _resid_mm(o, bf(w_o[j]), jnp.zeros((D_MODEL,), F32), x)
        if j == 0:
            (h,) = _norm(x, gains=(g_ffn[1],))
            y = _ffn(h, dense_expert, dense_ok, bf(w_ffn_gate[1])[None], bf(w_ffn_up[1])[None],
                     bf(w_ffn_down[1])[None])
            x, h = _norm(x, ys=(y,), gains=(g_attn[1],))
        else:
            y0, y1, gates = _moe(x, g_moe[1], w_router[1], bf(w_moe_gate[1]), bf(w_moe_up[1]), bf(w_moe_down[1]))
            (x,) = _norm(x, ys=(y0, y1), gates=gates)

    y_prompt = x[:N_PROMPT].reshape(BATCH, SEQ, D_MODEL)
    y_sample = x[ROW_SAMPLE:ROW_SAMPLE + N_SAMPLE].reshape(DEC_BATCH, DEC_SEQ, D_MODEL)
    kv4 = lambda t: t.reshape(*t.shape[:-1], N_KV_HEADS, HEAD_DIM)
    meta_k_p = jnp.broadcast_to(kv4(k_meta_p)[None], (BATCH, N_META, N_KV_HEADS, HEAD_DIM))
    meta_v_p = jnp.broadcast_to(kv4(v_meta_p)[None], (BATCH, N_META, N_KV_HEADS, HEAD_DIM))
    win_k_s = jnp.concatenate([cache_win_k, kv4(k_new)], axis=1)[:, -WINDOW:]
    win_v_s = jnp.concatenate([cache_win_v, kv4(v_new)], axis=1)[:, -WINDOW:]
    return (y_prompt, y_sample, jnp.stack(conv_p), jnp.stack(conv_s), meta_k_p, meta_v_p,
            kv4(k_fr[:, -WINDOW:]), kv4(v_fr[:, -WINDOW:]), win_k_s, win_v_s)
```

```python
import functools
import math

import jax
import jax.numpy as jnp
from jax import lax
from jax.experimental import pallas as pl
from jax.experimental.pallas import tpu as pltpu

D_MODEL = 2048
BATCH = 8
SEQ = 4096
DEPTH = 4
DEC_BATCH = 8
DEC_SEQ = 16
PAST_LEN = 4096
CHUNK = 64
N_META = 16
N_A = DEPTH // 2
CONV_W = 31
HEAD_DIM = 64
N_HEADS = D_MODEL // HEAD_DIM
N_KV_HEADS = N_HEADS // 8
GROUP = N_HEADS // N_KV_HEADS
WINDOW = 128
WIN_CHUNKS = WINDOW // CHUNK
ROPE_THETA = 10000.0
D_FF = 5632
N_EXPERTS = 8
TOP_K = 2
D_EXPERT = 7168
EPS = 1e-6
NEG_INF = -1e30

F32 = jnp.float32
BF16 = jnp.bfloat16

N_PROMPT = BATCH * SEQ
N_SAMPLE = DEC_BATCH * DEC_SEQ
N_SMALL = 256
M_ROWS = N_PROMPT + N_SMALL
ROW_SAMPLE = N_PROMPT
ROW_META = N_PROMPT + N_SAMPLE

TM = 768
TN = 1024
TF = 512
TB = TM
N_ASSIGN = M_ROWS * TOP_K
N_BLOCKS = N_ASSIGN // TB + N_EXPERTS
TR = 384
SLAB = 256
HIST = 32
CONV_TT = 128
CONV_RG = 64
CONV_LS = 256
FFN_SUB = 256
KV_W = N_KV_HEADS * HEAD_DIM
KEYS = 256
BAND = (WIN_CHUNKS + 1) * CHUNK
ATT_QB = 8
VMEM_LIMIT = 56 * 1024 * 1024


def _cparams(sem):
    return pltpu.CompilerParams(dimension_semantics=sem, vmem_limit_bytes=VMEM_LIMIT)


def _dot(a, b):
    return jnp.dot(a, b, preferred_element_type=F32)


def _norm_kernel(*refs, n_y, gated, n_g, write_x):
    x_ref = refs[0]
    y_refs = refs[1:1 + n_y]
    pos = 1 + n_y
    gate_ref = refs[pos] if gated else None
    pos += 1 if gated else 0
    g_refs = refs[pos:pos + n_g]
    pos += n_g
    outs = refs[pos:]
    x = x_ref[...]
    for i, y_ref in enumerate(y_refs):
        y = y_ref[...]
        x = x + (gate_ref[:, i:i + 1] * y if gated else y)
    o = 0
    if write_x:
        outs[0][...] = x
        o = 1
    if n_g:
        xn = x * lax.rsqrt(jnp.mean(x * x, axis=-1, keepdims=True) + EPS)
        for i, g_ref in enumerate(g_refs):
            outs[o + i][...] = (xn * g_ref[...]).astype(BF16)


def _norm(x, ys=(), gates=None, gains=(), *, tr=TR, row0=0, n_rows=M_ROWS):
    n_y, n_g = len(ys), len(gains)
    write_x = n_y > 0
    blk0 = row0 // tr
    row_in = pl.BlockSpec((tr, D_MODEL), lambda i: (blk0 + i, 0))
    row_out = pl.BlockSpec((tr, D_MODEL), lambda i: (i, 0))
    in_specs = [row_in] * (1 + n_y)
    args = [x, *ys]
    if gates is not None:
        in_specs.append(pl.BlockSpec((tr, 128), lambda i: (blk0 + i, 0)))
        args.append(gates)
    for g in gains:
        in_specs.append(pl.BlockSpec((1, D_MODEL), lambda i: (0, 0)))
        args.append(g.reshape(1, D_MODEL))
    out_shape, out_specs = [], []
    if write_x:
        out_shape.append(jax.ShapeDtypeStruct((n_rows, D_MODEL), F32))
        out_specs.append(row_out)
    for _ in gains:
        out_shape.append(jax.ShapeDtypeStruct((n_rows, D_MODEL), BF16))
        out_specs.append(row_out)
    return pl.pallas_call(
        functools.partial(_norm_kernel, n_y=n_y, gated=gates is not None, n_g=n_g, write_x=write_x),
        grid=(n_rows // tr,), in_specs=in_specs, out_specs=out_specs, out_shape=out_shape,
        compiler_params=_cparams(("parallel",)), name="norm")(*args)


def _glu_kernel(h_ref, wa_ref, wb_ref, ba_ref, bb_ref, u_ref):
    h = h_ref[...]
    a = _dot(h, wa_ref[0]) + ba_ref[...]
    b = _dot(h, wb_ref[0]) + bb_ref[...]
    u_ref[...] = a * jax.nn.sigmoid(b)


def _glu(h, w, b, layer):
    nj = D_MODEL // TN
    b2 = b.reshape(1, 2 * D_MODEL)
    return pl.pallas_call(
        _glu_kernel, grid=(M_ROWS // TM, nj),
        in_specs=[pl.BlockSpec((TM, D_MODEL), lambda i, j: (i, 0)),
                  pl.BlockSpec((1, D_MODEL, TN), lambda i, j: (layer, 0, j)),
                  pl.BlockSpec((1, D_MODEL, TN), lambda i, j: (layer, 0, j + nj)),
                  pl.BlockSpec((1, TN), lambda i, j: (0, j)),
                  pl.BlockSpec((1, TN), lambda i, j: (0, j + nj))],
        out_specs=pl.BlockSpec((TM, TN), lambda i, j: (i, j)),
        out_shape=jax.ShapeDtypeStruct((M_ROWS, D_MODEL), F32),
        compiler_params=_cparams(("parallel", "parallel")), name="pw1_glu")(h, w, w, b2, b2)


def _resid_kernel(x_ref, w_ref, b_ref, r_ref, o_ref):
    o_ref[...] = r_ref[...] + (_dot(x_ref[...], w_ref[0]) + b_ref[...])


def _resid_mm(x, w, b, r, layer):
    return pl.pallas_call(
        _resid_kernel, grid=(M_ROWS // TM, D_MODEL // TN),
        in_specs=[pl.BlockSpec((TM, D_MODEL), lambda i, j: (i, 0)),
                  pl.BlockSpec((1, D_MODEL, TN), lambda i, j: (layer, 0, j)),
                  pl.BlockSpec((1, TN), lambda i, j: (0, j)),
                  pl.BlockSpec((TM, TN), lambda i, j: (i, j))],
        out_specs=pl.BlockSpec((TM, TN), lambda i, j: (i, j)),
        out_shape=jax.ShapeDtypeStruct((M_ROWS, D_MODEL), F32),
        compiler_params=_cparams(("parallel", "parallel")), name="proj_resid")(x, w, b.reshape(1, D_MODEL), r)


def _head_norm_rope(x, gain, cos, sin, bd):
    ssq = _dot((x * x).astype(BF16), bd)
    xn = x * lax.rsqrt(ssq * (1.0 / HEAD_DIM) + EPS) * gain
    lane = lax.broadcasted_iota(jnp.int32, xn.shape, 1)
    first_half = (lane % HEAD_DIM) < (HEAD_DIM // 2)
    rot = jnp.where(first_half, pltpu.roll(xn, SLAB - HEAD_DIM // 2, 1), pltpu.roll(xn, HEAD_DIM // 2, 1))
    return xn * cos + rot * sin


def _q_kernel(h_ref, w_ref, g_ref, cos_ref, sin_ref, bd_ref, q_ref):
    bd, gain = bd_ref[...], g_ref[...]
    scale = HEAD_DIM ** -0.5
    for r in range(TM // FFN_SUB):
        rows = slice(r * FFN_SUB, (r + 1) * FFN_SUB)
        acc = _dot(h_ref[rows, :], w_ref[0])
        cos, sin = cos_ref[rows, :], sin_ref[rows, :]
        for s in range(TN // SLAB):
            q = _head_norm_rope(acc[:, s * SLAB:(s + 1) * SLAB], gain, cos, sin, bd)
            q_ref[rows, s * SLAB:(s + 1) * SLAB] = (q * scale).astype(BF16)


def _queries(h, w, g_q, cos, sin, bd, layer):
    return pl.pallas_call(
        _q_kernel, grid=(M_ROWS // TM, D_MODEL // TN),
        in_specs=[pl.BlockSpec((TM, D_MODEL), lambda i, j: (i, 0)),
                  pl.BlockSpec((1, D_MODEL, TN), lambda i, j: (layer, 0, j)),
                  pl.BlockSpec((1, SLAB), lambda i, j: (0, 0)),
                  pl.BlockSpec((TM, SLAB), lambda i, j: (i, 0)),
                  pl.BlockSpec((TM, SLAB), lambda i, j: (i, 0)),
                  pl.BlockSpec((SLAB, SLAB), lambda i, j: (0, 0))],
        out_specs=pl.BlockSpec((TM, TN), lambda i, j: (i, j)),
        out_shape=jax.ShapeDtypeStruct((M_ROWS, D_MODEL), BF16),
        compiler_params=_cparams(("parallel", "parallel")), name="q_proj")(
            h, w, jnp.tile(g_q, SLAB // HEAD_DIM).reshape(1, SLAB), cos, sin, bd)


def _kv_kernel(h_ref, w_ref, g_ref, cos_ref, sin_ref, bd_ref, k_ref, v_ref):
    acc = _dot(h_ref[...], w_ref[...])
    k_ref[...] = _head_norm_rope(acc[:, :KV_W], g_ref[...], cos_ref[...], sin_ref[...], bd_ref[...])
    v_ref[...] = acc[:, KV_W:]


def _shared_kv(h, w, g_k, cos, sin, bd):
    return pl.pallas_call(
        _kv_kernel, grid=(M_ROWS // TM,),
        in_specs=[pl.BlockSpec((TM, D_MODEL), lambda i: (i, 0)),
                  pl.BlockSpec((D_MODEL, 2 * KV_W), lambda i: (0, 0)),
                  pl.BlockSpec((1, SLAB), lambda i: (0, 0)),
                  pl.BlockSpec((TM, SLAB), lambda i: (i, 0)),
                  pl.BlockSpec((TM, SLAB), lambda i: (i, 0)),
                  pl.BlockSpec((SLAB, SLAB), lambda i: (0, 0))],
        out_specs=[pl.BlockSpec((TM, KV_W), lambda i: (i, 0))] * 2,
        out_shape=[jax.ShapeDtypeStruct((M_ROWS, KV_W), F32)] * 2,
        compiler_params=_cparams(("parallel",)), name="kv_proj")(
            h, w, jnp.tile(g_k, SLAB // HEAD_DIM).reshape(1, SLAB), cos, sin, bd)


def _conv_kernel(*refs, tt, use_prev):
    if use_prev:
        u_ref, prev_ref, hist_ref, w_ref, bdw_ref, gln_ref, bln_ref, c_ref, ext_ref, acc_ref = refs
    else:
        u_ref, hist_ref, w_ref, bdw_ref, gln_ref, bln_ref, c_ref, ext_ref, acc_ref = refs
    if use_prev:
        first = pl.program_id(1) == 0

        @pl.when(first)
        def _():
            ext_ref[0:HIST, :] = hist_ref[0]

        @pl.when(jnp.logical_not(first))
        def _():
            ext_ref[0:HIST, :] = prev_ref[...]
    else:
        ext_ref[0:HIST, :] = hist_ref[0]
    ext_ref[HIST:HIST + tt, :] = u_ref[...]
    rg = min(CONV_RG, tt)
    off = HIST - (CONV_W - 1)
    for r in range(tt // rg):
        for s in range(D_MODEL // CONV_LS):
            ls = slice(s * CONV_LS, (s + 1) * CONV_LS)
            acc = jnp.zeros((rg, CONV_LS), F32)
            for res in range(8):
                offs = [o for o in range(off, off + CONV_W) if o % 8 == res]
                win = ext_ref[r * rg + offs[0]:r * rg + offs[-1] + rg, ls]
                part = None
                for o in offs:
                    k = o - off
                    term = win[o - offs[0]:o - offs[0] + rg, :] * w_ref[k:k + 1, ls]
                    part = term if part is None else part + term
                acc = acc + part
            acc_ref[r * rg:(r + 1) * rg, ls] = acc
    x = acc_ref[...] + bdw_ref[...]
    xc = x - jnp.mean(x, axis=-1, keepdims=True)
    y = xc * lax.rsqrt(jnp.mean(xc * xc, axis=-1, keepdims=True) + EPS)
    y = y * gln_ref[...] + bln_ref[...]
    c_ref[...] = (y * jax.nn.sigmoid(y)).astype(BF16)


def _conv(u, hist, w_dw, b_dw, g_ln, b_ln, *, row0, n_seq, seq_len, tt):
    n_t = seq_len // tt
    use_prev = n_t > 1
    blk0 = row0 // tt
    wpad = jnp.zeros((32, D_MODEL), F32).at[:CONV_W].set(w_dw)
    vec = pl.BlockSpec((1, D_MODEL), lambda s, i: (0, 0))
    in_specs = [pl.BlockSpec((tt, D_MODEL), lambda s, i: (blk0 + s * n_t + i, 0))]
    args = [u]
    if use_prev:
        per = tt // HIST
        in_specs.append(pl.BlockSpec(
            (HIST, D_MODEL), lambda s, i: (jnp.maximum((blk0 + s * n_t + i) * per - 1, 0), 0)))
        args.append(u)
    in_specs += [pl.BlockSpec((1, HIST, D_MODEL), lambda s, i: (s, 0, 0)),
                 pl.BlockSpec((32, D_MODEL), lambda s, i: (0, 0)), vec, vec, vec]
    args += [hist, wpad, b_dw.reshape(1, D_MODEL), g_ln.reshape(1, D_MODEL), b_ln.reshape(1, D_MODEL)]
    return pl.pallas_call(
        functools.partial(_conv_kernel, tt=tt, use_prev=use_prev), grid=(n_seq, n_t),
        in_specs=in_specs,
        out_specs=pl.BlockSpec((tt, D_MODEL), lambda s, i: (s * n_t + i, 0)),
        out_shape=jax.ShapeDtypeStruct((n_seq * seq_len, D_MODEL), BF16),
        scratch_shapes=[pltpu.VMEM((HIST + tt, D_MODEL), F32), pltpu.VMEM((tt, D_MODEL), F32)],
        compiler_params=_cparams(("parallel", "arbitrary")), name="conv_ln_swish")(*args)


def _ffn_kernel(be_ref, ok_ref, x_ref, wg_ref, wu_ref, wd_ref, o_ref, *, n_f):
    i, f = pl.program_id(0), pl.program_id(1)
    ok = ok_ref[i] == 1

    @pl.when(f == 0)
    def _():
        o_ref[...] = jnp.zeros_like(o_ref)

    @pl.when(ok)
    def _():
        for r in range(TB // FFN_SUB):
            rows = slice(r * FFN_SUB, (r + 1) * FFN_SUB)
            x = x_ref[rows, :]
            g = _dot(x, wg_ref[0, 0])
            u = _dot(x, wu_ref[0, 0])
            a = (g * jax.nn.sigmoid(g) * u).astype(BF16)
            o_ref[rows, :] += _dot(a, wd_ref[0, 0])


def _ffn(x, block_expert, block_ok, wg, wu, wd, layer):
    n_blocks = x.shape[0] // TB
    d_hidden = wg.shape[-1]
    n_f = d_hidden // TF

    def f_idx(i, f, ok):
        return jnp.where(ok[i] == 1, f, n_f - 1)

    grid_spec = pltpu.PrefetchScalarGridSpec(
        num_scalar_prefetch=2, grid=(n_blocks, n_f),
        in_specs=[pl.BlockSpec((TB, D_MODEL), lambda i, f, be, ok: (i, 0)),
                  pl.BlockSpec((1, 1, D_MODEL, TF), lambda i, f, be, ok: (layer, be[i], 0, f_idx(i, f, ok))),
                  pl.BlockSpec((1, 1, D_MODEL, TF), lambda i, f, be, ok: (layer, be[i], 0, f_idx(i, f, ok))),
                  pl.BlockSpec((1, 1, TF, D_MODEL), lambda i, f, be, ok: (layer, be[i], f_idx(i, f, ok), 0))],
        out_specs=pl.BlockSpec((TB, D_MODEL), lambda i, f, be, ok: (i, 0)))
    return pl.pallas_call(
        functools.partial(_ffn_kernel, n_f=n_f), grid_spec=grid_spec,
        out_shape=jax.ShapeDtypeStruct((n_blocks * TB, D_MODEL), F32),
        compiler_params=_cparams(("parallel", "arbitrary")), name="swiglu")(
            block_expert, block_ok, x, wg, wu, wd)


def _router_kernel(x_ref, g_ref, wr_ref, h_ref, idx_ref, gate_ref):
    x = x_ref[...]
    h = x * lax.rsqrt(jnp.mean(x * x, axis=-1, keepdims=True) + EPS) * g_ref[...]
    h_ref[...] = h.astype(BF16)
    logits = jnp.dot(h, wr_ref[...], preferred_element_type=F32, precision=lax.Precision.HIGHEST)
    lane_i = lax.broadcasted_iota(jnp.int32, logits.shape, 1)
    lane = lane_i.astype(F32)
    logits = jnp.where(lane_i < N_EXPERTS, logits, -jnp.inf)
    m1 = jnp.max(logits, axis=-1, keepdims=True)
    i1 = jnp.min(jnp.where(logits == m1, lane, 128.0), axis=-1, keepdims=True)
    rest = jnp.where(lane == i1, -jnp.inf, logits)
    m2 = jnp.max(rest, axis=-1, keepdims=True)
    i2 = jnp.min(jnp.where(rest == m2, lane, 128.0), axis=-1, keepdims=True)
    e2 = jnp.exp(m2 - m1)
    g1 = 1.0 / (1.0 + e2)
    idx_ref[...] = jnp.where(lane_i == 0, i1, i2).astype(jnp.int32)
    gate_ref[...] = jnp.where(lane_i == 0, g1, e2 * g1)


def _router(x, g, w_router):
    wr = jnp.zeros((D_MODEL, 128), F32).at[:, :N_EXPERTS].set(w_router)
    return pl.pallas_call(
        _router_kernel, grid=(M_ROWS // TR,),
        in_specs=[pl.BlockSpec((TR, D_MODEL), lambda i: (i, 0)),
                  pl.BlockSpec((1, D_MODEL), lambda i: (0, 0)),
                  pl.BlockSpec((D_MODEL, 128), lambda i: (0, 0))],
        out_specs=[pl.BlockSpec((TR, D_MODEL), lambda i: (i, 0)),
                   pl.BlockSpec((TR, 128), lambda i: (i, 0)),
                   pl.BlockSpec((TR, 128), lambda i: (i, 0))],
        out_shape=[jax.ShapeDtypeStruct((M_ROWS, D_MODEL), BF16),
                   jax.ShapeDtypeStruct((M_ROWS, 128), jnp.int32),
                   jax.ShapeDtypeStruct((M_ROWS, 128), F32)],
        compiler_params=_cparams(("parallel",)), name="router")(x, g.reshape(1, D_MODEL), wr)


def _moe(x, g, w_router, wg, wu, wd, layer):
    h, idx, gates = _router(x, g, w_router)
    e_flat = idx[:, :TOP_K].reshape(N_ASSIGN)
    onehot = (e_flat[:, None] == jnp.arange(N_EXPERTS, dtype=jnp.int32)[None, :]).astype(jnp.int32)
    csum = jnp.cumsum(onehot, axis=0)
    rank = jnp.sum(csum * onehot, axis=-1) - 1
    counts = csum[-1]
    padded = (counts + TB - 1) // TB * TB
    pad_ends = jnp.cumsum(padded)
    pad_starts = pad_ends - padded
    dest = pad_starts[e_flat] + rank
    blk_start = jnp.arange(N_BLOCKS, dtype=jnp.int32) * TB
    block_expert = jnp.minimum(jnp.searchsorted(pad_ends, blk_start, side='right'),
                               N_EXPERTS - 1).astype(jnp.int32)
    block_ok = (blk_start < pad_ends[-1]).astype(jnp.int32)
    src = jnp.zeros((N_BLOCKS * TB,), jnp.int32).at[dest].set(jnp.arange(N_ASSIGN, dtype=jnp.int32) // TOP_K)
    y_buf = _ffn(h[src], block_expert, block_ok, wg, wu, wd, layer)
    d2 = dest.reshape(M_ROWS, TOP_K)
    return y_buf[d2[:, 0]], y_buf[d2[:, 1]], gates


def _attn_kernel(q_ref, k_ref, v_ref, km_ref, vm_ref, sink_ref, o_ref, *, tq, qb, hi, prompt):
    lane = lax.broadcasted_iota(jnp.int32, (KEYS, 128), 1)
    low = lane < HEAD_DIM
    key = lax.broadcasted_iota(jnp.int32, (1, 2 * KEYS), 1) % KEYS
    is_meta = jnp.logical_and(key >= BAND, key < BAND + N_META)
    zpad_k = jnp.zeros((KEYS - BAND - N_META, 128), BF16)
    ones_row = lax.broadcasted_iota(jnp.int32, (2 * KEYS, 128), 0) < KEYS
    ones_low = lax.broadcasted_iota(jnp.int32, (2 * KEYS, 128), 1) < HEAD_DIM
    sum_cols = jnp.where(ones_row == ones_low, 1.0, 0.0).astype(BF16)
    out_low = lax.broadcasted_iota(jnp.int32, (4 * tq, 128), 1) < HEAD_DIM

    def chunk(cc, carry):
        c_glob = pl.program_id(1) * qb + cc
        row0 = pl.multiple_of(cc * tq, tq)
        key0 = pl.multiple_of(c_glob * CHUNK, CHUNK) if prompt else 0
        lo = jnp.maximum(WIN_CHUNKS - c_glob, 0) * CHUNK if prompt else 0
        ok = jnp.logical_or(jnp.logical_and(key >= lo, key < hi), is_meta)
        bias = jnp.where(ok, 0.0, NEG_INF)
        for g in range(N_KV_HEADS):
            gl = slice(g * 128, (g + 1) * 128)
            k_all = jnp.concatenate([k_ref[0, pl.ds(key0, BAND), gl], km_ref[0, :, gl], zpad_k], axis=0)
            v_all = jnp.concatenate([v_ref[0, pl.ds(key0, BAND), gl], vm_ref[0, :, gl], zpad_k], axis=0)
            k2 = jnp.concatenate([jnp.where(low, k_all, 0), jnp.where(low, 0, k_all)], axis=0)
            v2 = jnp.concatenate([jnp.where(low, v_all, 0), jnp.where(low, 0, v_all)], axis=0)
            qp = jnp.concatenate(
                [q_ref[pl.ds(row0, tq), g * 512 + i * 128:g * 512 + (i + 1) * 128] for i in range(4)], axis=0)
            s = lax.dot_general(qp, k2, (((1,), (1,)), ((), ())), preferred_element_type=F32) + bias
            es, sink_e = [], []
            for p in range(2):
                sp = s[:, p * KEYS:(p + 1) * KEYS]
                sk = sink_ref[g, :, p:p + 1]
                m = jnp.maximum(jnp.max(sp, axis=-1, keepdims=True), sk)
                es.append(jnp.exp(sp - m).astype(BF16))
                sink_e.append(jnp.exp(sk - m))
            ox = _dot(jnp.concatenate(es, axis=1), jnp.concatenate([v2, sum_cols], axis=1))
            den = ox[:, 128:] + jnp.where(out_low, sink_e[0], sink_e[1])
            o = ox[:, :128] * (1.0 / den)
            for i in range(4):
                o_ref[pl.ds(row0, tq), g * 512 + i * 128:g * 512 + (i + 1) * 128] = (
                    o[i * tq:(i + 1) * tq, :].astype(BF16))
        return carry

    lax.fori_loop(0, qb, chunk, 0, unroll=min(2, qb))


def _attention(q, kd, vd, kmd, vmd, sinks, *, row0, n_b, n_rows, tq, qb, hi, prompt):
    blk_rows = tq * qb
    n_i = n_rows // blk_rows
    blk0 = row0 // blk_rows
    per_b_meta = kmd.shape[0] > 1
    sk = sinks.reshape(N_KV_HEADS, 4, 1, 2)
    sk = jnp.broadcast_to(sk, (N_KV_HEADS, 4, tq, 2)).reshape(N_KV_HEADS, 4 * tq, 2)
    n_keys = kd.shape[1]
    kv_spec = pl.BlockSpec((1, n_keys, 512), lambda b, i: (b, 0, 0))
    meta_spec = pl.BlockSpec((1, N_META, 512), lambda b, i: (b if per_b_meta else 0, 0, 0))
    return pl.pallas_call(
        functools.partial(_attn_kernel, tq=tq, qb=qb, hi=hi, prompt=prompt), grid=(n_b, n_i),
        in_specs=[pl.BlockSpec((blk_rows, D_MODEL), lambda b, i: (blk0 + b * n_i + i, 0)),
                  kv_spec, kv_spec, meta_spec, meta_spec,
                  pl.BlockSpec((N_KV_HEADS, 4 * tq, 2), lambda b, i: (0, 0, 0))],
        out_specs=pl.BlockSpec((blk_rows, D_MODEL), lambda b, i: (b * n_i + i, 0)),
        out_shape=jax.ShapeDtypeStruct((n_b * n_rows, D_MODEL), BF16),
        compiler_params=_cparams(("parallel", "arbitrary")), name="swa")(q, kd, vd, kmd, vmd, sk)


def _dup_heads(x):
    lead = x.shape[:-1]
    x = x.astype(BF16).reshape(*lead, N_KV_HEADS, 1, HEAD_DIM)
    return jnp.broadcast_to(x, (*lead, N_KV_HEADS, 2, HEAD_DIM)).reshape(*lead, 2 * KV_W)


def kernel(x_prompt, x_sample, state_conv, cache_meta_k, cache_meta_v, cache_win_k, cache_win_v, meta_tokens, g_conv, w_pw1, b_pw1, w_dw, b_dw, g_conv_ln, b_conv_ln, w_pw2, b_pw2, g_kv, w_kv, g_k, g_attn, w_q, g_q, sinks, w_o, g_ffn, w_ffn_gate, w_ffn_up, w_ffn_down, g_moe, w_router, w_moe_gate, w_moe_up, w_moe_down):
    bf = lambda w: w.astype(BF16)
    x = jnp.concatenate([x_prompt.reshape(N_PROMPT, D_MODEL), x_sample.reshape(N_SAMPLE, D_MODEL),
                         meta_tokens, jnp.zeros((M_ROWS - ROW_META - N_META, D_MODEL), F32)], axis=0)

    pos = jnp.concatenate([jnp.tile(N_META + jnp.arange(SEQ), BATCH),
                           jnp.tile(N_META + PAST_LEN + jnp.arange(DEC_SEQ), DEC_BATCH),
                           jnp.arange(N_META), jnp.zeros((M_ROWS - ROW_META - N_META,), jnp.int32)])
    half = HEAD_DIM // 2
    inv = jnp.exp(jnp.arange(half, dtype=F32) * (-2.0 * math.log(ROPE_THETA) / HEAD_DIM))
    ang = pos.astype(F32)[:, None] * inv[None, :]
    cos_t = jnp.tile(jnp.concatenate([jnp.cos(ang), jnp.cos(ang)], axis=-1), (1, SLAB // HEAD_DIM))
    sin_t = jnp.tile(jnp.concatenate([-jnp.sin(ang), jnp.sin(ang)], axis=-1), (1, SLAB // HEAD_DIM))
    hd = jnp.arange(SLAB) // HEAD_DIM
    bd = (hd[:, None] == hd[None, :]).astype(BF16)

    dense_expert = jnp.zeros((M_ROWS // TB,), jnp.int32)
    dense_ok = jnp.ones((M_ROWS // TB,), jnp.int32)
    conv_p, conv_s = [], []
    w_pw1, w_pw2, w_q, w_o, w_kv = bf(w_pw1), bf(w_pw2), bf(w_q), bf(w_o), bf(w_kv)
    w_ffn_gate, w_ffn_up, w_ffn_down = bf(w_ffn_gate)[:, None], bf(w_ffn_up)[:, None], bf(w_ffn_down)[:, None]
    w_moe_gate, w_moe_up, w_moe_down = bf(w_moe_gate), bf(w_moe_up), bf(w_moe_down)

    (h,) = _norm(x, gains=(g_conv[0],))
    for l in range(N_A):
        u = _glu(h, w_pw1, b_pw1[l], l)
        u_meta = u[ROW_META:ROW_META + N_META]
        u_samp = u[ROW_SAMPLE:ROW_SAMPLE + N_SAMPLE].reshape(DEC_BATCH, DEC_SEQ, D_MODEL)
        hist_p = jnp.broadcast_to(
            jnp.concatenate([jnp.zeros((HIST - N_META, D_MODEL), F32), u_meta], axis=0)[None],
            (BATCH, HIST, D_MODEL))
        hist_s = jnp.zeros((N_SMALL // DEC_SEQ, HIST, D_MODEL), F32).at[:DEC_BATCH, HIST - (CONV_W - 1):].set(
            state_conv[l])
        conv_args = (w_dw[l], b_dw[l], g_conv_ln[l], b_conv_ln[l])
        c = jnp.concatenate([
            _conv(u, hist_p, *conv_args, row0=0, n_seq=BATCH, seq_len=SEQ, tt=CONV_TT),
            _conv(u, hist_s, *conv_args, row0=ROW_SAMPLE, n_seq=N_SMALL // DEC_SEQ, seq_len=DEC_SEQ,
                  tt=DEC_SEQ)], axis=0)
        x = _resid_mm(c, w_pw2, b_pw2[l], x, l)
        conv_p.append(jnp.stack([u[(b + 1) * SEQ - (CONV_W - 1):(b + 1) * SEQ] for b in range(BATCH)]))
        conv_s.append(jnp.concatenate([state_conv[l], u_samp], axis=1)[:, -(CONV_W - 1):])
        if l == 0:
            (h,) = _norm(x, gains=(g_ffn[0],))
            y = _ffn(h, dense_expert, dense_ok, w_ffn_gate, w_ffn_up, w_ffn_down, 0)
            x, h = _norm(x, ys=(y,), gains=(g_conv[1],))
        else:
            y0, y1, gates = _moe(x, g_moe[0], w_router[0], w_moe_gate, w_moe_up, w_moe_down, 0)
            x, h_kv, h = _norm(x, ys=(y0, y1), gates=gates, gains=(g_kv, g_attn[0]))

    k, v = _shared_kv(h_kv, w_kv, g_k, cos_t, sin_t, bd)
    k_fr = k[:N_PROMPT].reshape(BATCH, SEQ, KV_W)
    v_fr = v[:N_PROMPT].reshape(BATCH, SEQ, KV_W)
    k_new = k[ROW_SAMPLE:ROW_SAMPLE + N_SAMPLE].reshape(DEC_BATCH, DEC_SEQ, KV_W)
    v_new = v[ROW_SAMPLE:ROW_SAMPLE + N_SAMPLE].reshape(DEC_BATCH, DEC_SEQ, KV_W)
    k_meta_p = k[ROW_META:ROW_META + N_META]
    v_meta_p = v[ROW_META:ROW_META + N_META]
    front = jnp.zeros((BATCH, WINDOW, KV_W), F32)
    kd_p = _dup_heads(jnp.concatenate([front, k_fr], axis=1))
    vd_p = _dup_heads(jnp.concatenate([front, v_fr], axis=1))
    tail = jnp.zeros((DEC_BATCH, BAND - WINDOW - DEC_SEQ, KV_W), F32)
    kd_s = _dup_heads(jnp.concatenate([cache_win_k.reshape(DEC_BATCH, WINDOW, KV_W), k_new, tail], axis=1))
    vd_s = _dup_heads(jnp.concatenate([cache_win_v.reshape(DEC_BATCH, WINDOW, KV_W), v_new, tail], axis=1))
    kmd_p, vmd_p = _dup_heads(k_meta_p)[None], _dup_heads(v_meta_p)[None]
    kmd_s = _dup_heads(cache_meta_k.reshape(DEC_BATCH, N_META, KV_W))
    vmd_s = _dup_heads(cache_meta_v.reshape(DEC_BATCH, N_META, KV_W))
    o_rest = jnp.zeros((M_ROWS - ROW_META, D_MODEL), BF16)

    for j in range(DEPTH - N_A):
        q = _queries(h, w_q, g_q[j], cos_t, sin_t, bd, j)
        o = jnp.concatenate([
            _attention(q, kd_p, vd_p, kmd_p, vmd_p, sinks[j], row0=0, n_b=BATCH, n_rows=SEQ, tq=CHUNK,
                       qb=ATT_QB, hi=BAND, prompt=True),
            _attention(q, kd_s, vd_s, kmd_s, vmd_s, sinks[j], row0=ROW_SAMPLE, n_b=DEC_BATCH,
                       n_rows=DEC_SEQ, tq=DEC_SEQ, qb=1, hi=WINDOW + DEC_SEQ, prompt=False),
            o_rest], axis=0)
        x = _resid_mm(o, w_o, jnp.zeros((D_MODEL,), F32), x, j)
        if j == 0:
            (h,) = _norm(x, gains=(g_ffn[1],))
            y = _ffn(h, dense_expert, dense_ok, w_ffn_gate, w_ffn_up, w_ffn_down, 1)
            x, h = _norm(x, ys=(y,), gains=(g_attn[1],))
        else:
            y0, y1, gates = _moe(x, g_moe[1], w_router[1], w_moe_gate, w_moe_up, w_moe_down, 1)
            (y_prompt,) = _norm(x, ys=(y0, y1), gates=gates, tr=256, row0=0, n_rows=N_PROMPT)
            (y_sample,) = _norm(x, ys=(y0, y1), gates=gates, tr=N_SAMPLE, row0=ROW_SAMPLE, n_rows=N_SAMPLE)

    y_prompt = y_prompt.reshape(BATCH, SEQ, D_MODEL)
    y_sample = y_sample.reshape(DEC_BATCH, DEC_SEQ, D_MODEL)
    kv4 = lambda t: t.reshape(*t.shape[:-1], N_KV_HEADS, HEAD_DIM)
    meta_k_p = jnp.broadcast_to(kv4(k_meta_p)[None], (BATCH, N_META, N_KV_HEADS, HEAD_DIM))
    meta_v_p = jnp.broadcast_to(kv4(v_meta_p)[None], (BATCH, N_META, N_KV_HEADS, HEAD_DIM))
    win_k_s = jnp.concatenate([cache_win_k, kv4(k_new)], axis=1)[:, -WINDOW:]
    win_v_s = jnp.concatenate([cache_win_v, kv4(v_new)], axis=1)[:, -WINDOW:]
    return (y_prompt, y_sample, jnp.stack(conv_p), jnp.stack(conv_s), meta_k_p, meta_v_p,
            kv4(k_fr[:, -WINDOW:]), kv4(v_fr[:, -WINDOW:]), win_k_s, win_v_s)
```

```python
import functools
import math

import jax
import jax.numpy as jnp
from jax import lax
from jax.experimental import pallas as pl
from jax.experimental.pallas import tpu as pltpu

D_MODEL = 2048
BATCH = 8
SEQ = 4096
DEPTH = 4
DEC_BATCH = 8
DEC_SEQ = 16
PAST_LEN = 4096
CHUNK = 64
N_META = 16
N_A = DEPTH // 2
CONV_W = 31
HEAD_DIM = 64
N_HEADS = D_MODEL // HEAD_DIM
N_KV_HEADS = N_HEADS // 8
GROUP = N_HEADS // N_KV_HEADS
WINDOW = 128
WIN_CHUNKS = WINDOW // CHUNK
ROPE_THETA = 10000.0
D_FF = 5632
N_EXPERTS = 8
TOP_K = 2
D_EXPERT = 7168
EPS = 1e-6
NEG_INF = -1e30

F32 = jnp.float32
BF16 = jnp.bfloat16

N_PROMPT = BATCH * SEQ
N_SAMPLE = DEC_BATCH * DEC_SEQ
N_SMALL = 256
M_ROWS = N_PROMPT + N_SMALL
ROW_SAMPLE = N_PROMPT
ROW_META = N_PROMPT + N_SAMPLE

TM = 768
TN = 1024
TF = 512
TB = TM
N_ASSIGN = M_ROWS * TOP_K
N_BLOCKS = N_ASSIGN // TB + N_EXPERTS
TR = 384
SLAB = 256
HIST = 32
CONV_TT = 128
CONV_RG = 64
CONV_LS = 256
FFN_SUB = 384
Q_SUB = 256
CAST_ROWS = 256
KV_W = N_KV_HEADS * HEAD_DIM
KEYS = 256
BAND = (WIN_CHUNKS + 1) * CHUNK
ATT_QB = 8
VMEM_LIMIT = 56 * 1024 * 1024


def _cparams(sem):
    return pltpu.CompilerParams(dimension_semantics=sem, vmem_limit_bytes=VMEM_LIMIT)


def _dot(a, b):
    return jnp.dot(a, b, preferred_element_type=F32)


def _norm_kernel(*refs, n_y, gated, n_g, write_x):
    x_ref = refs[0]
    y_refs = refs[1:1 + n_y]
    pos = 1 + n_y
    gate_ref = refs[pos] if gated else None
    pos += 1 if gated else 0
    g_refs = refs[pos:pos + n_g]
    pos += n_g
    outs = refs[pos:]
    x = x_ref[...]
    for i, y_ref in enumerate(y_refs):
        y = y_ref[...]
        x = x + (gate_ref[:, i:i + 1] * y if gated else y)
    o = 0
    if write_x:
        outs[0][...] = x
        o = 1
    if n_g:
        xn = x * lax.rsqrt(jnp.mean(x * x, axis=-1, keepdims=True) + EPS)
        for i, g_ref in enumerate(g_refs):
            outs[o + i][...] = (xn * g_ref[...]).astype(BF16)


def _norm(x, ys=(), gates=None, gains=(), *, tr=TR, row0=0, n_rows=M_ROWS):
    n_y, n_g = len(ys), len(gains)
    write_x = n_y > 0
    blk0 = row0 // tr
    row_in = pl.BlockSpec((tr, D_MODEL), lambda i: (blk0 + i, 0))
    row_out = pl.BlockSpec((tr, D_MODEL), lambda i: (i, 0))
    in_specs = [row_in] * (1 + n_y)
    args = [x, *ys]
    if gates is not None:
        in_specs.append(pl.BlockSpec((tr, 128), lambda i: (blk0 + i, 0)))
        args.append(gates)
    for g in gains:
        in_specs.append(pl.BlockSpec((1, D_MODEL), lambda i: (0, 0)))
        args.append(g.reshape(1, D_MODEL))
    out_shape, out_specs = [], []
    if write_x:
        out_shape.append(jax.ShapeDtypeStruct((n_rows, D_MODEL), F32))
        out_specs.append(row_out)
    for _ in gains:
        out_shape.append(jax.ShapeDtypeStruct((n_rows, D_MODEL), BF16))
        out_specs.append(row_out)
    return pl.pallas_call(
        functools.partial(_norm_kernel, n_y=n_y, gated=gates is not None, n_g=n_g, write_x=write_x),
        grid=(n_rows // tr,), in_specs=in_specs, out_specs=out_specs, out_shape=out_shape,
        compiler_params=_cparams(("parallel",)), name="norm")(*args)


def _glu_kernel(h_ref, wa_ref, wb_ref, ba_ref, bb_ref, u_ref):
    h = h_ref[...]
    a = _dot(h, wa_ref[0]) + ba_ref[...]
    b = _dot(h, wb_ref[0]) + bb_ref[...]
    u_ref[...] = a * jax.nn.sigmoid(b)


def _glu(h, w, b, layer):
    nj = D_MODEL // TN
    b2 = b.reshape(1, 2 * D_MODEL)
    return pl.pallas_call(
        _glu_kernel, grid=(nj, M_ROWS // TM),
        in_specs=[pl.BlockSpec((TM, D_MODEL), lambda j, i: (i, 0)),
                  pl.BlockSpec((1, D_MODEL, TN), lambda j, i: (layer, 0, j)),
                  pl.BlockSpec((1, D_MODEL, TN), lambda j, i: (layer, 0, j + nj)),
                  pl.BlockSpec((1, TN), lambda j, i: (0, j)),
                  pl.BlockSpec((1, TN), lambda j, i: (0, j + nj))],
        out_specs=pl.BlockSpec((TM, TN), lambda j, i: (i, j)),
        out_shape=jax.ShapeDtypeStruct((M_ROWS, D_MODEL), F32),
        compiler_params=_cparams(("parallel", "parallel")), name="pw1_glu")(h, w, w, b2, b2)


def _resid_kernel(x_ref, w_ref, b_ref, r_ref, o_ref):
    o_ref[...] = r_ref[...] + (_dot(x_ref[...], w_ref[0]) + b_ref[...])


def _resid_mm(x, w, b, r, layer):
    return pl.pallas_call(
        _resid_kernel, grid=(D_MODEL // TN, M_ROWS // TM),
        in_specs=[pl.BlockSpec((TM, D_MODEL), lambda j, i: (i, 0)),
                  pl.BlockSpec((1, D_MODEL, TN), lambda j, i: (layer, 0, j)),
                  pl.BlockSpec((1, TN), lambda j, i: (0, j)),
                  pl.BlockSpec((TM, TN), lambda j, i: (i, j))],
        out_specs=pl.BlockSpec((TM, TN), lambda j, i: (i, j)),
        out_shape=jax.ShapeDtypeStruct((M_ROWS, D_MODEL), F32),
        compiler_params=_cparams(("parallel", "parallel")), name="proj_resid")(x, w, b.reshape(1, D_MODEL), r)


def _head_norm_rope(x, gain, cos, sin, bd):
    ssq = _dot((x * x).astype(BF16), bd)
    xn = x * lax.rsqrt(ssq * (1.0 / HEAD_DIM) + EPS) * gain
    lane = lax.broadcasted_iota(jnp.int32, xn.shape, 1)
    first_half = (lane % HEAD_DIM) < (HEAD_DIM // 2)
    rot = jnp.where(first_half, pltpu.roll(xn, SLAB - HEAD_DIM // 2, 1), pltpu.roll(xn, HEAD_DIM // 2, 1))
    return xn * cos + rot * sin


def _q_kernel(h_ref, w_ref, g_ref, cos_ref, sin_ref, bd_ref, q_ref):
    bd, gain = bd_ref[...], g_ref[...]
    scale = HEAD_DIM ** -0.5
    for r in range(TM // Q_SUB):
        rows = slice(r * Q_SUB, (r + 1) * Q_SUB)
        acc = _dot(h_ref[rows, :], w_ref[0])
        cos, sin = cos_ref[rows, :], sin_ref[rows, :]
        for s in range(TN // SLAB):
            q = _head_norm_rope(acc[:, s * SLAB:(s + 1) * SLAB], gain, cos, sin, bd)
            q_ref[rows, s * SLAB:(s + 1) * SLAB] = (q * scale).astype(BF16)


def _queries(h, w, g_q, cos, sin, bd, layer):
    return pl.pallas_call(
        _q_kernel, grid=(D_MODEL // TN, M_ROWS // TM),
        in_specs=[pl.BlockSpec((TM, D_MODEL), lambda j, i: (i, 0)),
                  pl.BlockSpec((1, D_MODEL, TN), lambda j, i: (layer, 0, j)),
                  pl.BlockSpec((1, SLAB), lambda j, i: (0, 0)),
                  pl.BlockSpec((TM, SLAB), lambda j, i: (i, 0)),
                  pl.BlockSpec((TM, SLAB), lambda j, i: (i, 0)),
                  pl.BlockSpec((SLAB, SLAB), lambda j, i: (0, 0))],
        out_specs=pl.BlockSpec((TM, TN), lambda j, i: (i, j)),
        out_shape=jax.ShapeDtypeStruct((M_ROWS, D_MODEL), BF16),
        compiler_params=_cparams(("parallel", "parallel")), name="q_proj")(
            h, w, jnp.tile(g_q, SLAB // HEAD_DIM).reshape(1, SLAB), cos, sin, bd)


def _kv_kernel(h_ref, w_ref, g_ref, cos_ref, sin_ref, bd_ref, k_ref, v_ref):
    acc = _dot(h_ref[...], w_ref[...])
    k_ref[...] = _head_norm_rope(acc[:, :KV_W], g_ref[...], cos_ref[...], sin_ref[...], bd_ref[...])
    v_ref[...] = acc[:, KV_W:]


def _shared_kv(h, w, g_k, cos, sin, bd):
    return pl.pallas_call(
        _kv_kernel, grid=(M_ROWS // TM,),
        in_specs=[pl.BlockSpec((TM, D_MODEL), lambda i: (i, 0)),
                  pl.BlockSpec((D_MODEL, 2 * KV_W), lambda i: (0, 0)),
                  pl.BlockSpec((1, SLAB), lambda i: (0, 0)),
                  pl.BlockSpec((TM, SLAB), lambda i: (i, 0)),
                  pl.BlockSpec((TM, SLAB), lambda i: (i, 0)),
                  pl.BlockSpec((SLAB, SLAB), lambda i: (0, 0))],
        out_specs=[pl.BlockSpec((TM, KV_W), lambda i: (i, 0))] * 2,
        out_shape=[jax.ShapeDtypeStruct((M_ROWS, KV_W), F32)] * 2,
        compiler_params=_cparams(("parallel",)), name="kv_proj")(
            h, w, jnp.tile(g_k, SLAB // HEAD_DIM).reshape(1, SLAB), cos, sin, bd)


def _conv_kernel(*refs, tt, use_prev, has_dst):
    c_ref, ext_ref, acc_ref = refs[-3:]
    ins = refs[:-4] if has_dst else refs[:-3]
    if use_prev:
        u_ref, prev_ref, hist_ref, w_ref, bdw_ref, gln_ref, bln_ref = ins
    else:
        u_ref, hist_ref, w_ref, bdw_ref, gln_ref, bln_ref = ins
    if use_prev:
        first = pl.program_id(1) == 0

        @pl.when(first)
        def _():
            ext_ref[0:HIST, :] = hist_ref[0]

        @pl.when(jnp.logical_not(first))
        def _():
            ext_ref[0:HIST, :] = prev_ref[...]
    else:
        ext_ref[0:HIST, :] = hist_ref[0]
    ext_ref[HIST:HIST + tt, :] = u_ref[...]
    rg = min(CONV_RG, tt)
    off = HIST - (CONV_W - 1)
    for r in range(tt // rg):
        for s in range(D_MODEL // CONV_LS):
            ls = slice(s * CONV_LS, (s + 1) * CONV_LS)
            acc = jnp.zeros((rg, CONV_LS), F32)
            for res in range(8):
                offs = [o for o in range(off, off + CONV_W) if o % 8 == res]
                win = ext_ref[r * rg + offs[0]:r * rg + offs[-1] + rg, ls]
                part = None
                for o in offs:
                    k = o - off
                    term = win[o - offs[0]:o - offs[0] + rg, :] * w_ref[k:k + 1, ls]
                    part = term if part is None else part + term
                acc = acc + part
            acc_ref[r * rg:(r + 1) * rg, ls] = acc
    x = acc_ref[...] + bdw_ref[...]
    xc = x - jnp.mean(x, axis=-1, keepdims=True)
    y = xc * lax.rsqrt(jnp.mean(xc * xc, axis=-1, keepdims=True) + EPS)
    y = y * gln_ref[...] + bln_ref[...]
    c_ref[...] = (y * jax.nn.sigmoid(y)).astype(BF16)


def _conv(u, hist, w_dw, b_dw, g_ln, b_ln, *, row0, n_seq, seq_len, tt, dst=None):
    n_t = seq_len // tt
    use_prev = n_t > 1
    blk0 = row0 // tt
    wpad = jnp.zeros((32, D_MODEL), F32).at[:CONV_W].set(w_dw)
    vec = pl.BlockSpec((1, D_MODEL), lambda s, i: (0, 0))
    in_specs = [pl.BlockSpec((tt, D_MODEL), lambda s, i: (blk0 + s * n_t + i, 0))]
    args = [u]
    if use_prev:
        per = tt // HIST
        in_specs.append(pl.BlockSpec(
            (HIST, D_MODEL), lambda s, i: (jnp.maximum((blk0 + s * n_t + i) * per - 1, 0), 0)))
        args.append(u)
    in_specs += [pl.BlockSpec((1, HIST, D_MODEL), lambda s, i: (s, 0, 0)),
                 pl.BlockSpec((32, D_MODEL), lambda s, i: (0, 0)), vec, vec, vec]
    args += [hist, wpad, b_dw.reshape(1, D_MODEL), g_ln.reshape(1, D_MODEL), b_ln.reshape(1, D_MODEL)]
    aliases = {}
    if dst is not None:
        aliases = {len(args): 0}
        in_specs.append(pl.BlockSpec(memory_space=pl.ANY))
        args.append(dst)
    return pl.pallas_call(
        functools.partial(_conv_kernel, tt=tt, use_prev=use_prev, has_dst=dst is not None), grid=(n_seq, n_t),
        in_specs=in_specs,
        out_specs=pl.BlockSpec((tt, D_MODEL), lambda s, i: (blk0 + s * n_t + i, 0)),
        out_shape=jax.ShapeDtypeStruct((M_ROWS, D_MODEL), BF16),
        input_output_aliases=aliases,
        scratch_shapes=[pltpu.VMEM((HIST + tt, D_MODEL), F32), pltpu.VMEM((tt, D_MODEL), F32)],
        compiler_params=_cparams(("parallel", "arbitrary")), name="conv_ln_swish")(*args)


def _ffn_kernel(be_ref, ok_ref, x_ref, wg_ref, wu_ref, wd_ref, o_ref):
    i, f = pl.program_id(0), pl.program_id(1)
    ok = ok_ref[i] == 1

    @pl.when(f == 0)
    def _():
        o_ref[...] = jnp.zeros_like(o_ref)

    @pl.when(ok)
    def _():
        for r in range(TB // FFN_SUB):
            rows = slice(r * FFN_SUB, (r + 1) * FFN_SUB)
            x = x_ref[rows, :]
            g = _dot(x, wg_ref[0, 0, 0])
            u = _dot(x, wu_ref[0, 0, 0])
            a = (g * jax.nn.sigmoid(g) * u).astype(BF16)
            o_ref[rows, :] += _dot(a, wd_ref[0, 0])


def _ffn(x, block_expert, block_ok, wg, wu, wd, layer):
    n_blocks = x.shape[0] // TB
    n_f = wg.shape[2]

    def f_idx(i, f, ok):
        return jnp.where(ok[i] == 1, f, n_f - 1)

    grid_spec = pltpu.PrefetchScalarGridSpec(
        num_scalar_prefetch=2, grid=(n_blocks, n_f),
        in_specs=[pl.BlockSpec((TB, D_MODEL), lambda i, f, be, ok: (i, 0)),
                  pl.BlockSpec((1, 1, 1, D_MODEL, TF),
                               lambda i, f, be, ok: (layer, be[i], f_idx(i, f, ok), 0, 0)),
                  pl.BlockSpec((1, 1, 1, D_MODEL, TF),
                               lambda i, f, be, ok: (layer, be[i], f_idx(i, f, ok), 0, 0)),
                  pl.BlockSpec((1, 1, TF, D_MODEL), lambda i, f, be, ok: (layer, be[i], f_idx(i, f, ok), 0))],
        out_specs=pl.BlockSpec((TB, D_MODEL), lambda i, f, be, ok: (i, 0)))
    return pl.pallas_call(
        _ffn_kernel, grid_spec=grid_spec,
        out_shape=jax.ShapeDtypeStruct((n_blocks * TB, D_MODEL), F32),
        compiler_params=_cparams(("parallel", "arbitrary")), name="swiglu")(
            block_expert, block_ok, x, wg, wu, wd)


def _router_kernel(x_ref, g_ref, wr_ref, h_ref, idx_ref, gate_ref):
    x = x_ref[...]
    h = x * lax.rsqrt(jnp.mean(x * x, axis=-1, keepdims=True) + EPS) * g_ref[...]
    h_ref[...] = h.astype(BF16)
    logits = jnp.dot(h, wr_ref[...], preferred_element_type=F32, precision=lax.Precision.HIGHEST)
    lane_i = lax.broadcasted_iota(jnp.int32, logits.shape, 1)
    lane = lane_i.astype(F32)
    logits = jnp.where(lane_i < N_EXPERTS, logits, -jnp.inf)
    m1 = jnp.max(logits, axis=-1, keepdims=True)
    i1 = jnp.min(jnp.where(logits == m1, lane, 128.0), axis=-1, keepdims=True)
    rest = jnp.where(lane == i1, -jnp.inf, logits)
    m2 = jnp.max(rest, axis=-1, keepdims=True)
    i2 = jnp.min(jnp.where(rest == m2, lane, 128.0), axis=-1, keepdims=True)
    e2 = jnp.exp(m2 - m1)
    g1 = 1.0 / (1.0 + e2)
    idx_ref[...] = jnp.where(lane_i == 0, i1, i2).astype(jnp.int32)
    gate_ref[...] = jnp.where(lane_i == 0, g1, e2 * g1)


def _router(x, g, w_router):
    wr = jnp.zeros((D_MODEL, 128), F32).at[:, :N_EXPERTS].set(w_router)
    return pl.pallas_call(
        _router_kernel, grid=(M_ROWS // TR,),
        in_specs=[pl.BlockSpec((TR, D_MODEL), lambda i: (i, 0)),
                  pl.BlockSpec((1, D_MODEL), lambda i: (0, 0)),
                  pl.BlockSpec((D_MODEL, 128), lambda i: (0, 0))],
        out_specs=[pl.BlockSpec((TR, D_MODEL), lambda i: (i, 0)),
                   pl.BlockSpec((TR, 128), lambda i: (i, 0)),
                   pl.BlockSpec((TR, 128), lambda i: (i, 0))],
        out_shape=[jax.ShapeDtypeStruct((M_ROWS, D_MODEL), BF16),
                   jax.ShapeDtypeStruct((M_ROWS, 128), jnp.int32),
                   jax.ShapeDtypeStruct((M_ROWS, 128), F32)],
        compiler_params=_cparams(("parallel",)), name="router")(x, g.reshape(1, D_MODEL), wr)


def _moe(x, g, w_router, wg, wu, wd, layer):
    h, idx, gates = _router(x, g, w_router)
    e_flat = idx[:, :TOP_K].reshape(N_ASSIGN)
    onehot = (e_flat[:, None] == jnp.arange(N_EXPERTS, dtype=jnp.int32)[None, :]).astype(jnp.int32)
    csum = jnp.cumsum(onehot, axis=0)
    rank = jnp.sum(csum * onehot, axis=-1) - 1
    counts = csum[-1]
    padded = (counts + TB - 1) // TB * TB
    pad_ends = jnp.cumsum(padded)
    pad_starts = pad_ends - padded
    dest = pad_starts[e_flat] + rank
    blk_start = jnp.arange(N_BLOCKS, dtype=jnp.int32) * TB
    block_expert = jnp.minimum(jnp.searchsorted(pad_ends, blk_start, side='right'),
                               N_EXPERTS - 1).astype(jnp.int32)
    block_ok = (blk_start < pad_ends[-1]).astype(jnp.int32)
    src = jnp.zeros((N_BLOCKS * TB,), jnp.int32).at[dest].set(jnp.arange(N_ASSIGN, dtype=jnp.int32) // TOP_K)
    y_buf = _ffn(h[src], block_expert, block_ok, wg, wu, wd, layer)
    d2 = dest.reshape(M_ROWS, TOP_K)
    return y_buf[d2[:, 0]], y_buf[d2[:, 1]], gates


def _attn_kernel(*refs, tq, qb, hi, prompt):
    q_ref, k_ref, v_ref, km_ref, vm_ref, sink_ref = refs[:6]
    o_ref = refs[-1]
    _attn_body(q_ref, k_ref, v_ref, km_ref, vm_ref, sink_ref, o_ref, tq=tq, qb=qb, hi=hi, prompt=prompt)


def _attn_body(q_ref, k_ref, v_ref, km_ref, vm_ref, sink_ref, o_ref, *, tq, qb, hi, prompt):
    lane = lax.broadcasted_iota(jnp.int32, (KEYS, 128), 1)
    low = lane < HEAD_DIM
    key = lax.broadcasted_iota(jnp.int32, (1, 2 * KEYS), 1) % KEYS
    is_meta = jnp.logical_and(key >= BAND, key < BAND + N_META)
    zpad_k = jnp.zeros((KEYS - BAND - N_META, 128), BF16)
    ones_row = lax.broadcasted_iota(jnp.int32, (2 * KEYS, 128), 0) < KEYS
    ones_low = lax.broadcasted_iota(jnp.int32, (2 * KEYS, 128), 1) < HEAD_DIM
    sum_cols = jnp.where(ones_row == ones_low, 1.0, 0.0).astype(BF16)
    out_low = lax.broadcasted_iota(jnp.int32, (4 * tq, 128), 1) < HEAD_DIM

    def chunk(cc, carry):
        c_glob = pl.program_id(1) * qb + cc
        row0 = pl.multiple_of(cc * tq, tq)
        key0 = pl.multiple_of(c_glob * CHUNK, CHUNK) if prompt else 0
        lo = jnp.maximum(WIN_CHUNKS - c_glob, 0) * CHUNK if prompt else 0
        ok = jnp.logical_or(jnp.logical_and(key >= lo, key < hi), is_meta)
        bias = jnp.where(ok, 0.0, NEG_INF)
        for g in range(N_KV_HEADS):
            gl = slice(g * 128, (g + 1) * 128)
            k_all = jnp.concatenate([k_ref[0, pl.ds(key0, BAND), gl], km_ref[0, :, gl], zpad_k], axis=0)
            v_all = jnp.concatenate([v_ref[0, pl.ds(key0, BAND), gl], vm_ref[0, :, gl], zpad_k], axis=0)
            k2 = jnp.concatenate([jnp.where(low, k_all, 0), jnp.where(low, 0, k_all)], axis=0)
            v2 = jnp.concatenate([jnp.where(low, v_all, 0), jnp.where(low, 0, v_all)], axis=0)
            qp = jnp.concatenate(
                [q_ref[pl.ds(row0, tq), g * 512 + i * 128:g * 512 + (i + 1) * 128] for i in range(4)], axis=0)
            s = lax.dot_general(qp, k2, (((1,), (1,)), ((), ())), preferred_element_type=F32) + bias
            es, sink_e = [], []
            for p in range(2):
                sp = s[:, p * KEYS:(p + 1) * KEYS]
                sk = sink_ref[g, :, p:p + 1]
                m = jnp.maximum(jnp.max(sp, axis=-1, keepdims=True), sk)
                es.append(jnp.exp(sp - m).astype(BF16))
                sink_e.append(jnp.exp(sk - m))
            ox = _dot(jnp.concatenate(es, axis=1), jnp.concatenate([v2, sum_cols], axis=1))
            den = ox[:, 128:] + jnp.where(out_low, sink_e[0], sink_e[1])
            o = ox[:, :128] * (1.0 / den)
            for i in range(4):
                o_ref[pl.ds(row0, tq), g * 512 + i * 128:g * 512 + (i + 1) * 128] = (
                    o[i * tq:(i + 1) * tq, :].astype(BF16))
        return carry

    lax.fori_loop(0, qb, chunk, 0, unroll=min(2, qb))


def _attention(q, kd, vd, kmd, vmd, sinks, *, row0, n_b, n_rows, tq, qb, hi, prompt, dst=None):
    blk_rows = tq * qb
    n_i = n_rows // blk_rows
    blk0 = row0 // blk_rows
    per_b_meta = kmd.shape[0] > 1
    sk = sinks.reshape(N_KV_HEADS, 4, 1, 2)
    sk = jnp.broadcast_to(sk, (N_KV_HEADS, 4, tq, 2)).reshape(N_KV_HEADS, 4 * tq, 2)
    n_keys = kd.shape[1]
    kv_spec = pl.BlockSpec((1, n_keys, 512), lambda b, i: (b, 0, 0))
    meta_spec = pl.BlockSpec((1, N_META, 512), lambda b, i: (b if per_b_meta else 0, 0, 0))
    row_spec = pl.BlockSpec((blk_rows, D_MODEL), lambda b, i: (blk0 + b * n_i + i, 0))
    in_specs = [row_spec, kv_spec, kv_spec, meta_spec, meta_spec,
                pl.BlockSpec((N_KV_HEADS, 4 * tq, 2), lambda b, i: (0, 0, 0))]
    args = [q, kd, vd, kmd, vmd, sk]
    aliases = {}
    if dst is not None:
        aliases = {len(args): 0}
        in_specs.append(pl.BlockSpec(memory_space=pl.ANY))
        args.append(dst)
    return pl.pallas_call(
        functools.partial(_attn_kernel, tq=tq, qb=qb, hi=hi, prompt=prompt),
        grid=(n_b, n_i), in_specs=in_specs, out_specs=row_spec,
        out_shape=jax.ShapeDtypeStruct((M_ROWS, D_MODEL), BF16), input_output_aliases=aliases,
        compiler_params=_cparams(("parallel", "arbitrary")), name="swa")(*args)


def _cast_kernel(w_ref, o_ref):
    o_ref[...] = w_ref[...].astype(BF16)


def _cast_rows(w):
    n_l, n_e, d_hidden, d_out = w.shape
    spec = pl.BlockSpec((1, TF, d_out), lambda e, i: (e, i, 0))
    out = pl.pallas_call(
        _cast_kernel, grid=(n_l * n_e, d_hidden // TF), in_specs=[spec], out_specs=spec,
        out_shape=jax.ShapeDtypeStruct((n_l * n_e, d_hidden, d_out), BF16),
        compiler_params=_cparams(("parallel", "parallel")), name="cast")(w.reshape(n_l * n_e, d_hidden, d_out))
    return out.reshape(n_l, n_e, d_hidden, d_out)


def _tile_cols_kernel(w_ref, o_ref):
    for f in range(o_ref.shape[1]):
        o_ref[0, f] = w_ref[0, :, f * TF:(f + 1) * TF].astype(BF16)


def _tile_cols(w):
    n_l, n_e, d_in, d_hidden = w.shape
    n_f = d_hidden // TF
    out = pl.pallas_call(
        _tile_cols_kernel, grid=(n_l * n_e, d_in // CAST_ROWS),
        in_specs=[pl.BlockSpec((1, CAST_ROWS, d_hidden), lambda e, i: (e, i, 0))],
        out_specs=pl.BlockSpec((1, n_f, CAST_ROWS, TF), lambda e, i: (e, 0, i, 0)),
        out_shape=jax.ShapeDtypeStruct((n_l * n_e, n_f, d_in, TF), BF16),
        compiler_params=_cparams(("parallel", "parallel")), name="cast_tile")(w.reshape(n_l * n_e, d_in, d_hidden))
    return out.reshape(n_l, n_e, n_f, d_in, TF)


def _dup_heads(x):
    lead = x.shape[:-1]
    x = x.astype(BF16).reshape(*lead, N_KV_HEADS, 1, HEAD_DIM)
    return jnp.broadcast_to(x, (*lead, N_KV_HEADS, 2, HEAD_DIM)).reshape(*lead, 2 * KV_W)


def kernel(x_prompt, x_sample, state_conv, cache_meta_k, cache_meta_v, cache_win_k, cache_win_v, meta_tokens, g_conv, w_pw1, b_pw1, w_dw, b_dw, g_conv_ln, b_conv_ln, w_pw2, b_pw2, g_kv, w_kv, g_k, g_attn, w_q, g_q, sinks, w_o, g_ffn, w_ffn_gate, w_ffn_up, w_ffn_down, g_moe, w_router, w_moe_gate, w_moe_up, w_moe_down):
    bf = lambda w: w.astype(BF16)
    x = jnp.concatenate([x_prompt.reshape(N_PROMPT, D_MODEL), x_sample.reshape(N_SAMPLE, D_MODEL),
                         meta_tokens, jnp.zeros((M_ROWS - ROW_META - N_META, D_MODEL), F32)], axis=0)

    pos = jnp.concatenate([jnp.tile(N_META + jnp.arange(SEQ), BATCH),
                           jnp.tile(N_META + PAST_LEN + jnp.arange(DEC_SEQ), DEC_BATCH),
                           jnp.arange(N_META), jnp.zeros((M_ROWS - ROW_META - N_META,), jnp.int32)])
    half = HEAD_DIM // 2
    inv = jnp.exp(jnp.arange(half, dtype=F32) * (-2.0 * math.log(ROPE_THETA) / HEAD_DIM))
    ang = pos.astype(F32)[:, None] * inv[None, :]
    cos_t = jnp.tile(jnp.concatenate([jnp.cos(ang), jnp.cos(ang)], axis=-1), (1, SLAB // HEAD_DIM))
    sin_t = jnp.tile(jnp.concatenate([-jnp.sin(ang), jnp.sin(ang)], axis=-1), (1, SLAB // HEAD_DIM))
    hd = jnp.arange(SLAB) // HEAD_DIM
    bd = (hd[:, None] == hd[None, :]).astype(BF16)

    dense_expert = jnp.zeros((M_ROWS // TB,), jnp.int32)
    dense_ok = jnp.ones((M_ROWS // TB,), jnp.int32)
    conv_p, conv_s = [], []
    w_pw1, w_pw2, w_q, w_o, w_kv = bf(w_pw1), bf(w_pw2), bf(w_q), bf(w_o), bf(w_kv)
    w_ffn_gate, w_ffn_up = _tile_cols(w_ffn_gate[:, None]), _tile_cols(w_ffn_up[:, None])
    w_moe_gate, w_moe_up = _tile_cols(w_moe_gate), _tile_cols(w_moe_up)
    w_ffn_down, w_moe_down = _cast_rows(w_ffn_down[:, None]), _cast_rows(w_moe_down)

    (h,) = _norm(x, gains=(g_conv[0],))
    for l in range(N_A):
        u = _glu(h, w_pw1, b_pw1[l], l)
        u_meta = u[ROW_META:ROW_META + N_META]
        u_samp = u[ROW_SAMPLE:ROW_SAMPLE + N_SAMPLE].reshape(DEC_BATCH, DEC_SEQ, D_MODEL)
        hist_p = jnp.broadcast_to(
            jnp.concatenate([jnp.zeros((HIST - N_META, D_MODEL), F32), u_meta], axis=0)[None],
            (BATCH, HIST, D_MODEL))
        hist_s = jnp.zeros((N_SMALL // DEC_SEQ, HIST, D_MODEL), F32).at[:DEC_BATCH, HIST - (CONV_W - 1):].set(
            state_conv[l])
        conv_args = (w_dw[l], b_dw[l], g_conv_ln[l], b_conv_ln[l])
        c = _conv(u, hist_p, *conv_args, row0=0, n_seq=BATCH, seq_len=SEQ, tt=CONV_TT)
        c = _conv(u, hist_s, *conv_args, row0=ROW_SAMPLE, n_seq=N_SMALL // DEC_SEQ, seq_len=DEC_SEQ,
                  tt=DEC_SEQ, dst=c)
        x = _resid_mm(c, w_pw2, b_pw2[l], x, l)
        conv_p.append(jnp.stack([u[(b + 1) * SEQ - (CONV_W - 1):(b + 1) * SEQ] for b in range(BATCH)]))
        conv_s.append(jnp.concatenate([state_conv[l], u_samp], axis=1)[:, -(CONV_W - 1):])
        if l == 0:
            (h,) = _norm(x, gains=(g_ffn[0],))
            y = _ffn(h, dense_expert, dense_ok, w_ffn_gate, w_ffn_up, w_ffn_down, 0)
            x, h = _norm(x, ys=(y,), gains=(g_conv[1],))
        else:
            y0, y1, gates = _moe(x, g_moe[0], w_router[0], w_moe_gate, w_moe_up, w_moe_down, 0)
            x, h_kv, h = _norm(x, ys=(y0, y1), gates=gates, gains=(g_kv, g_attn[0]))

    k, v = _shared_kv(h_kv, w_kv, g_k, cos_t, sin_t, bd)
    k_fr = k[:N_PROMPT].reshape(BATCH, SEQ, KV_W)
    v_fr = v[:N_PROMPT].reshape(BATCH, SEQ, KV_W)
    k_new = k[ROW_SAMPLE:ROW_SAMPLE + N_SAMPLE].reshape(DEC_BATCH, DEC_SEQ, KV_W)
    v_new = v[ROW_SAMPLE:ROW_SAMPLE + N_SAMPLE].reshape(DEC_BATCH, DEC_SEQ, KV_W)
    k_meta_p = k[ROW_META:ROW_META + N_META]
    v_meta_p = v[ROW_META:ROW_META + N_META]
    front = jnp.zeros((BATCH, WINDOW, KV_W), F32)
    kd_p = _dup_heads(jnp.concatenate([front, k_fr], axis=1))
    vd_p = _dup_heads(jnp.concatenate([front, v_fr], axis=1))
    tail = jnp.zeros((DEC_BATCH, BAND - WINDOW - DEC_SEQ, KV_W), F32)
    pad_b = lambda t: jnp.concatenate(
        [t, jnp.zeros((N_SMALL // DEC_SEQ - DEC_BATCH, *t.shape[1:]), t.dtype)], axis=0)
    kd_s = _dup_heads(pad_b(jnp.concatenate([cache_win_k.reshape(DEC_BATCH, WINDOW, KV_W), k_new, tail], axis=1)))
    vd_s = _dup_heads(pad_b(jnp.concatenate([cache_win_v.reshape(DEC_BATCH, WINDOW, KV_W), v_new, tail], axis=1)))
    kmd_p, vmd_p = _dup_heads(k_meta_p)[None], _dup_heads(v_meta_p)[None]
    kmd_s = _dup_heads(pad_b(cache_meta_k.reshape(DEC_BATCH, N_META, KV_W)))
    vmd_s = _dup_heads(pad_b(cache_meta_v.reshape(DEC_BATCH, N_META, KV_W)))

    for j in range(DEPTH - N_A):
        q = _queries(h, w_q, g_q[j], cos_t, sin_t, bd, j)
        o = _attention(q, kd_p, vd_p, kmd_p, vmd_p, sinks[j], row0=0, n_b=BATCH, n_rows=SEQ, tq=CHUNK,
                       qb=ATT_QB, hi=BAND, prompt=True)
        o = _attention(q, kd_s, vd_s, kmd_s, vmd_s, sinks[j], row0=ROW_SAMPLE, n_b=N_SMALL // DEC_SEQ,
                       n_rows=DEC_SEQ, tq=DEC_SEQ, qb=1, hi=WINDOW + DEC_SEQ, prompt=False, dst=o)
        x = _resid_mm(o, w_o, jnp.zeros((D_MODEL,), F32), x, j)
        if j == 0:
            (h,) = _norm(x, gains=(g_ffn[1],))
            y = _ffn(h, dense_expert, dense_ok, w_ffn_gate, w_ffn_up, w_ffn_down, 1)
            x, h = _norm(x, ys=(y,), gains=(g_attn[1],))
        else:
            y0, y1, gates = _moe(x, g_moe[1], w_router[1], w_moe_gate, w_moe_up, w_moe_down, 1)
            (y_prompt,) = _norm(x, ys=(y0, y1), gates=gates, tr=256, row0=0, n_rows=N_PROMPT)
            (y_sample,) = _norm(x, ys=(y0, y1), gates=gates, tr=N_SAMPLE, row0=ROW_SAMPLE, n_rows=N_SAMPLE)

    y_prompt = y_prompt.reshape(BATCH, SEQ, D_MODEL)
    y_sample = y_sample.reshape(DEC_BATCH, DEC_SEQ, D_MODEL)
    kv4 = lambda t: t.reshape(*t.shape[:-1], N_KV_HEADS, HEAD_DIM)
    meta_k_p = jnp.broadcast_to(kv4(k_meta_p)[None], (BATCH, N_META, N_KV_HEADS, HEAD_DIM))
    meta_v_p = jnp.broadcast_to(kv4(v_meta_p)[None], (BATCH, N_META, N_KV_HEADS, HEAD_DIM))
    win_k_s = jnp.concatenate([cache_win_k, kv4(k_new)], axis=1)[:, -WINDOW:]
    win_v_s = jnp.concatenate([cache_win_v, kv4(v_new)], axis=1)[:, -WINDOW:]
    return (y_prompt, y_sample, jnp.stack(conv_p), jnp.stack(conv_s), meta_k_p, meta_v_p,
            kv4(k_fr[:, -WINDOW:]), kv4(v_fr[:, -WINDOW:]), win_k_s, win_v_s)
```

```python
import functools
import math

import jax
import jax.numpy as jnp
from jax import lax
from jax.experimental import pallas as pl
from jax.experimental.pallas import tpu as pltpu

D_MODEL = 2048
BATCH = 8
SEQ = 4096
DEPTH = 4
DEC_BATCH = 8
DEC_SEQ = 16
PAST_LEN = 4096
CHUNK = 64
N_META = 16
N_A = DEPTH // 2
CONV_W = 31
HEAD_DIM = 64
N_HEADS = D_MODEL // HEAD_DIM
N_KV_HEADS = N_HEADS // 8
GROUP = N_HEADS // N_KV_HEADS
WINDOW = 128
WIN_CHUNKS = WINDOW // CHUNK
ROPE_THETA = 10000.0
D_FF = 5632
N_EXPERTS = 8
TOP_K = 2
D_EXPERT = 7168
EPS = 1e-6
NEG_INF = -1e30

F32 = jnp.float32
BF16 = jnp.bfloat16

N_PROMPT = BATCH * SEQ
N_SAMPLE = DEC_BATCH * DEC_SEQ
N_SMALL = 256
M_ROWS = N_PROMPT + N_SMALL
ROW_SAMPLE = N_PROMPT
ROW_META = N_PROMPT + N_SAMPLE

TM = 768
TN = 1024
TF = 512
TB = TM
N_ASSIGN = M_ROWS * TOP_K
N_BLOCKS = N_ASSIGN // TB + N_EXPERTS
TR = 384
SLAB = 256
HIST = 32
CONV_TT = 128
CONV_RG = 128
CONV_LS = 128
FFN_SUBS = (384, 384)
TMR = 384
R_SUB = 128
Q_SUB = 256
CAST_ROWS = 256
KV_W = N_KV_HEADS * HEAD_DIM
KEYS = 256
BAND = (WIN_CHUNKS + 1) * CHUNK
ATT_QB = 8
VMEM_LIMIT = 56 * 1024 * 1024


def _cparams(sem):
    return pltpu.CompilerParams(dimension_semantics=sem, vmem_limit_bytes=VMEM_LIMIT)


def _dot(a, b):
    return jnp.dot(a, b, preferred_element_type=F32)


def _norm_kernel(*refs, n_y, gated, n_g, write_x):
    x_ref = refs[0]
    y_refs = refs[1:1 + n_y]
    pos = 1 + n_y
    gate_ref = refs[pos] if gated else None
    pos += 1 if gated else 0
    g_refs = refs[pos:pos + n_g]
    pos += n_g
    outs = refs[pos:]
    x = x_ref[...]
    for i, y_ref in enumerate(y_refs):
        y = y_ref[...]
        x = x + (gate_ref[:, i:i + 1] * y if gated else y)
    o = 0
    if write_x:
        outs[0][...] = x
        o = 1
    if n_g:
        xn = x * lax.rsqrt(jnp.mean(x * x, axis=-1, keepdims=True) + EPS)
        for i, g_ref in enumerate(g_refs):
            outs[o + i][...] = (xn * g_ref[...]).astype(BF16)


def _norm(x, ys=(), gates=None, gains=(), *, tr=TR, row0=0, n_rows=M_ROWS):
    n_y, n_g = len(ys), len(gains)
    write_x = n_y > 0
    blk0 = row0 // tr
    row_in = pl.BlockSpec((tr, D_MODEL), lambda i: (blk0 + i, 0))
    row_out = pl.BlockSpec((tr, D_MODEL), lambda i: (i, 0))
    in_specs = [row_in] * (1 + n_y)
    args = [x, *ys]
    if gates is not None:
        in_specs.append(pl.BlockSpec((tr, 128), lambda i: (blk0 + i, 0)))
        args.append(gates)
    for g in gains:
        in_specs.append(pl.BlockSpec((1, D_MODEL), lambda i: (0, 0)))
        args.append(g.reshape(1, D_MODEL))
    out_shape, out_specs = [], []
    if write_x:
        out_shape.append(jax.ShapeDtypeStruct((n_rows, D_MODEL), F32))
        out_specs.append(row_out)
    for _ in gains:
        out_shape.append(jax.ShapeDtypeStruct((n_rows, D_MODEL), BF16))
        out_specs.append(row_out)
    return pl.pallas_call(
        functools.partial(_norm_kernel, n_y=n_y, gated=gates is not None, n_g=n_g, write_x=write_x),
        grid=(n_rows // tr,), in_specs=in_specs, out_specs=out_specs, out_shape=out_shape,
        compiler_params=_cparams(("parallel",)), name="norm")(*args)


def _glu_kernel(h_ref, wa_ref, wb_ref, ba_ref, bb_ref, u_ref):
    h = h_ref[...]
    a = _dot(h, wa_ref[0]) + ba_ref[...]
    b = _dot(h, wb_ref[0]) + bb_ref[...]
    u_ref[...] = a * jax.nn.sigmoid(b)


def _glu(h, w, b, layer):
    nj = D_MODEL // TN
    b2 = b.reshape(1, 2 * D_MODEL)
    return pl.pallas_call(
        _glu_kernel, grid=(nj, M_ROWS // TM),
        in_specs=[pl.BlockSpec((TM, D_MODEL), lambda j, i: (i, 0)),
                  pl.BlockSpec((1, D_MODEL, TN), lambda j, i: (layer, 0, j)),
                  pl.BlockSpec((1, D_MODEL, TN), lambda j, i: (layer, 0, j + nj)),
                  pl.BlockSpec((1, TN), lambda j, i: (0, j)),
                  pl.BlockSpec((1, TN), lambda j, i: (0, j + nj))],
        out_specs=pl.BlockSpec((TM, TN), lambda j, i: (i, j)),
        out_shape=jax.ShapeDtypeStruct((M_ROWS, D_MODEL), F32),
        compiler_params=_cparams(("parallel", "parallel")), name="pw1_glu")(h, w, w, b2, b2)


def _top2(h, wr_ref, idx_ref, gate_ref, rows):
    h_hi = h.astype(BF16)
    h_lo = (h - h_hi.astype(F32)).astype(BF16)
    logits = _dot(h_hi, wr_ref[0]) + (_dot(h_hi, wr_ref[1]) + _dot(h_lo, wr_ref[0]))
    lane_i = lax.broadcasted_iota(jnp.int32, logits.shape, 1)
    lane = lane_i.astype(F32)
    logits = jnp.where(lane_i < N_EXPERTS, logits, -jnp.inf)
    m1 = jnp.max(logits, axis=-1, keepdims=True)
    i1 = jnp.min(jnp.where(logits == m1, lane, 128.0), axis=-1, keepdims=True)
    rest = jnp.where(lane == i1, -jnp.inf, logits)
    m2 = jnp.max(rest, axis=-1, keepdims=True)
    i2 = jnp.min(jnp.where(rest == m2, lane, 128.0), axis=-1, keepdims=True)
    e2 = jnp.exp(m2 - m1)
    g1 = 1.0 / (1.0 + e2)
    idx_ref[rows, :] = jnp.where(lane_i == 0, i1, i2).astype(jnp.int32)
    gate_ref[rows, :] = jnp.where(lane_i == 0, g1, e2 * g1)


def _resid_kernel(*refs, routed):
    if routed:
        x_ref, w_ref, b_ref, r_ref, g_ref, wr_ref, xo_ref, h_ref, idx_ref, gate_ref = refs
    else:
        x_ref, w_ref, b_ref, r_ref, g_ref, xo_ref, h_ref = refs
    for s in range(TMR // R_SUB):
        rows = slice(s * R_SUB, (s + 1) * R_SUB)
        xn = r_ref[rows, :] + (_dot(x_ref[rows, :], w_ref[0]) + b_ref[...])
        xo_ref[rows, :] = xn
        h = xn * lax.rsqrt(jnp.mean(xn * xn, axis=-1, keepdims=True) + EPS) * g_ref[...]
        h_ref[rows, :] = h.astype(BF16)
        if routed:
            _top2(h, wr_ref, idx_ref, gate_ref, rows)


def _resid_mm(x, w, b, r, layer, gain, w_router=None):
    routed = w_router is not None
    row = lambda width: pl.BlockSpec((TMR, width), lambda i: (i, 0))
    vec = pl.BlockSpec((1, D_MODEL), lambda i: (0, 0))
    in_specs = [row(D_MODEL), pl.BlockSpec((1, D_MODEL, D_MODEL), lambda i: (layer, 0, 0)), vec, row(D_MODEL), vec]
    args = [x, w, b.reshape(1, D_MODEL), r, gain.reshape(1, D_MODEL)]
    out_specs = [row(D_MODEL), row(D_MODEL)]
    out_shape = [jax.ShapeDtypeStruct((M_ROWS, D_MODEL), F32), jax.ShapeDtypeStruct((M_ROWS, D_MODEL), BF16)]
    if routed:
        in_specs.append(pl.BlockSpec((2, D_MODEL, 128), lambda i: (0, 0, 0)))
        wr = jnp.zeros((D_MODEL, 128), F32).at[:, :N_EXPERTS].set(w_router)
        wr_hi = wr.astype(BF16)
        args.append(jnp.stack([wr_hi, (wr - wr_hi.astype(F32)).astype(BF16)]))
        out_specs += [row(128), row(128)]
        out_shape += [jax.ShapeDtypeStruct((M_ROWS, 128), jnp.int32), jax.ShapeDtypeStruct((M_ROWS, 128), F32)]
    return pl.pallas_call(
        functools.partial(_resid_kernel, routed=routed), grid=(M_ROWS // TMR,),
        in_specs=in_specs, out_specs=out_specs, out_shape=out_shape,
        compiler_params=_cparams(("parallel",)), name="proj_resid_norm")(*args)


def _head_norm_rope(x, gain, cos, sin, bd):
    ssq = _dot((x * x).astype(BF16), bd)
    xn = x * lax.rsqrt(ssq * (1.0 / HEAD_DIM) + EPS) * gain
    lane = lax.broadcasted_iota(jnp.int32, xn.shape, 1)
    first_half = (lane % HEAD_DIM) < (HEAD_DIM // 2)
    rot = jnp.where(first_half, pltpu.roll(xn, SLAB - HEAD_DIM // 2, 1), pltpu.roll(xn, HEAD_DIM // 2, 1))
    return xn * cos + rot * sin


def _q_kernel(h_ref, w_ref, g_ref, cos_ref, sin_ref, bd_ref, q_ref):
    bd, gain = bd_ref[...], g_ref[...]
    scale = HEAD_DIM ** -0.5
    for r in range(TM // Q_SUB):
        rows = slice(r * Q_SUB, (r + 1) * Q_SUB)
        acc = _dot(h_ref[rows, :], w_ref[0])
        cos, sin = cos_ref[rows, :], sin_ref[rows, :]
        for s in range(TN // SLAB):
            q = _head_norm_rope(acc[:, s * SLAB:(s + 1) * SLAB], gain, cos, sin, bd)
            q_ref[rows, s * SLAB:(s + 1) * SLAB] = (q * scale).astype(BF16)


def _queries(h, w, g_q, cos, sin, bd, layer):
    return pl.pallas_call(
        _q_kernel, grid=(D_MODEL // TN, M_ROWS // TM),
        in_specs=[pl.BlockSpec((TM, D_MODEL), lambda j, i: (i, 0)),
                  pl.BlockSpec((1, D_MODEL, TN), lambda j, i: (layer, 0, j)),
                  pl.BlockSpec((1, SLAB), lambda j, i: (0, 0)),
                  pl.BlockSpec((TM, SLAB), lambda j, i: (i, 0)),
                  pl.BlockSpec((TM, SLAB), lambda j, i: (i, 0)),
                  pl.BlockSpec((SLAB, SLAB), lambda j, i: (0, 0))],
        out_specs=pl.BlockSpec((TM, TN), lambda j, i: (i, j)),
        out_shape=jax.ShapeDtypeStruct((M_ROWS, D_MODEL), BF16),
        compiler_params=_cparams(("parallel", "parallel")), name="q_proj")(
            h, w, jnp.tile(g_q, SLAB // HEAD_DIM).reshape(1, SLAB), cos, sin, bd)


def _kv_kernel(h_ref, w_ref, g_ref, cos_ref, sin_ref, bd_ref, k_ref, v_ref):
    acc = _dot(h_ref[...], w_ref[...])
    k_ref[...] = _head_norm_rope(acc[:, :KV_W], g_ref[...], cos_ref[...], sin_ref[...], bd_ref[...])
    v_ref[...] = acc[:, KV_W:]


def _shared_kv(h, w, g_k, cos, sin, bd):
    return pl.pallas_call(
        _kv_kernel, grid=(M_ROWS // TM,),
        in_specs=[pl.BlockSpec((TM, D_MODEL), lambda i: (i, 0)),
                  pl.BlockSpec((D_MODEL, 2 * KV_W), lambda i: (0, 0)),
                  pl.BlockSpec((1, SLAB), lambda i: (0, 0)),
                  pl.BlockSpec((TM, SLAB), lambda i: (i, 0)),
                  pl.BlockSpec((TM, SLAB), lambda i: (i, 0)),
                  pl.BlockSpec((SLAB, SLAB), lambda i: (0, 0))],
        out_specs=[pl.BlockSpec((TM, KV_W), lambda i: (i, 0))] * 2,
        out_shape=[jax.ShapeDtypeStruct((M_ROWS, KV_W), F32)] * 2,
        compiler_params=_cparams(("parallel",)), name="kv_proj")(
            h, w, jnp.tile(g_k, SLAB // HEAD_DIM).reshape(1, SLAB), cos, sin, bd)


def _conv_kernel(*refs, tt, use_prev, has_dst):
    c_ref, ext_ref, acc_ref = refs[-3:]
    ins = refs[:-4] if has_dst else refs[:-3]
    if use_prev:
        u_ref, prev_ref, hist_ref, w_ref, bdw_ref, gln_ref, bln_ref = ins
    else:
        u_ref, hist_ref, w_ref, bdw_ref, gln_ref, bln_ref = ins
    if use_prev:
        first = pl.program_id(1) == 0

        @pl.when(first)
        def _():
            ext_ref[0:HIST, :] = hist_ref[0]

        @pl.when(jnp.logical_not(first))
        def _():
            ext_ref[0:HIST, :] = prev_ref[...]
    else:
        ext_ref[0:HIST, :] = hist_ref[0]
    ext_ref[HIST:HIST + tt, :] = u_ref[...]
    rg = min(CONV_RG, tt)
    off = HIST - (CONV_W - 1)
    for r in range(tt // rg):
        for s in range(D_MODEL // CONV_LS):
            ls = slice(s * CONV_LS, (s + 1) * CONV_LS)
            acc = jnp.zeros((rg, CONV_LS), F32)
            for res in range(8):
                offs = [o for o in range(off, off + CONV_W) if o % 8 == res]
                win = ext_ref[r * rg + offs[0]:r * rg + offs[-1] + rg, ls]
                part = None
                for o in offs:
                    k = o - off
                    term = win[o - offs[0]:o - offs[0] + rg, :] * w_ref[k:k + 1, ls]
                    part = term if part is None else part + term
                acc = acc + part
            acc_ref[r * rg:(r + 1) * rg, ls] = acc
    x = acc_ref[...] + bdw_ref[...]
    xc = x - jnp.mean(x, axis=-1, keepdims=True)
    y = xc * lax.rsqrt(jnp.mean(xc * xc, axis=-1, keepdims=True) + EPS)
    y = y * gln_ref[...] + bln_ref[...]
    c_ref[...] = (y * jax.nn.sigmoid(y)).astype(BF16)


def _conv(u, hist, w_dw, b_dw, g_ln, b_ln, *, row0, n_seq, seq_len, tt, dst=None):
    n_t = seq_len // tt
    use_prev = n_t > 1
    blk0 = row0 // tt
    wpad = jnp.zeros((32, D_MODEL), F32).at[:CONV_W].set(w_dw)
    vec = pl.BlockSpec((1, D_MODEL), lambda s, i: (0, 0))
    in_specs = [pl.BlockSpec((tt, D_MODEL), lambda s, i: (blk0 + s * n_t + i, 0))]
    args = [u]
    if use_prev:
        per = tt // HIST
        in_specs.append(pl.BlockSpec(
            (HIST, D_MODEL), lambda s, i: (jnp.maximum((blk0 + s * n_t + i) * per - 1, 0), 0)))
        args.append(u)
    in_specs += [pl.BlockSpec((1, HIST, D_MODEL), lambda s, i: (s, 0, 0)),
                 pl.BlockSpec((32, D_MODEL), lambda s, i: (0, 0)), vec, vec, vec]
    args += [hist, wpad, b_dw.reshape(1, D_MODEL), g_ln.reshape(1, D_MODEL), b_ln.reshape(1, D_MODEL)]
    aliases = {}
    if dst is not None:
        aliases = {len(args): 0}
        in_specs.append(pl.BlockSpec(memory_space=pl.ANY))
        args.append(dst)
    return pl.pallas_call(
        functools.partial(_conv_kernel, tt=tt, use_prev=use_prev, has_dst=dst is not None), grid=(n_seq, n_t),
        in_specs=in_specs,
        out_specs=pl.BlockSpec((tt, D_MODEL), lambda s, i: (blk0 + s * n_t + i, 0)),
        out_shape=jax.ShapeDtypeStruct((M_ROWS, D_MODEL), BF16),
        input_output_aliases=aliases,
        scratch_shapes=[pltpu.VMEM((HIST + tt, D_MODEL), F32), pltpu.VMEM((tt, D_MODEL), F32)],
        compiler_params=_cparams(("parallel", "arbitrary")), name="conv_ln_swish")(*args)


def _ffn_kernel(be_ref, ok_ref, x_ref, wg_ref, wu_ref, wd_ref, o_ref, acc_ref):
    i, f = pl.program_id(0), pl.program_id(1)
    ok = ok_ref[i] == 1

    @pl.when(f == 0)
    def _():
        acc_ref[...] = jnp.zeros_like(acc_ref)

    @pl.when(ok)
    def _():
        r0 = 0
        for n in FFN_SUBS:
            rows = slice(r0, r0 + n)
            r0 += n
            x = x_ref[rows, :]
            g = _dot(x, wg_ref[0, 0, 0])
            u = _dot(x, wu_ref[0, 0, 0])
            a = (g * jax.nn.sigmoid(g) * u).astype(BF16)
            acc_ref[rows, :] += _dot(a, wd_ref[0, 0])

    @pl.when(f == pl.num_programs(1) - 1)
    def _():
        o_ref[...] = acc_ref[...].astype(BF16)


def _ffn(x, block_expert, block_ok, wg, wu, wd, layer):
    n_blocks = x.shape[0] // TB
    n_f = wg.shape[2]

    def f_idx(i, f, ok):
        return jnp.where(ok[i] == 1, f, n_f - 1)

    grid_spec = pltpu.PrefetchScalarGridSpec(
        num_scalar_prefetch=2, grid=(n_blocks, n_f),
        in_specs=[pl.BlockSpec((TB, D_MODEL), lambda i, f, be, ok: (i, 0)),
                  pl.BlockSpec((1, 1, 1, D_MODEL, TF),
                               lambda i, f, be, ok: (layer, be[i], f_idx(i, f, ok), 0, 0)),
                  pl.BlockSpec((1, 1, 1, D_MODEL, TF),
                               lambda i, f, be, ok: (layer, be[i], f_idx(i, f, ok), 0, 0)),
                  pl.BlockSpec((1, 1, TF, D_MODEL), lambda i, f, be, ok: (layer, be[i], f_idx(i, f, ok), 0))],
        out_specs=pl.BlockSpec((TB, D_MODEL), lambda i, f, be, ok: (i, 0)),
        scratch_shapes=[pltpu.VMEM((TB, D_MODEL), F32)])
    return pl.pallas_call(
        _ffn_kernel, grid_spec=grid_spec,
        out_shape=jax.ShapeDtypeStruct((n_blocks * TB, D_MODEL), BF16),
        compiler_params=_cparams(("parallel", "arbitrary")), name="swiglu")(
            block_expert, block_ok, x, wg, wu, wd)


def _moe(h, idx, wg, wu, wd, layer):
    e_flat = idx[:, :TOP_K].reshape(N_ASSIGN)
    onehot = (e_flat[:, None] == jnp.arange(N_EXPERTS, dtype=jnp.int32)[None, :]).astype(jnp.int32)
    csum = jnp.cumsum(onehot, axis=0)
    rank = jnp.sum(csum * onehot, axis=-1) - 1
    counts = csum[-1]
    padded = (counts + TB - 1) // TB * TB
    pad_ends = jnp.cumsum(padded)
    pad_starts = pad_ends - padded
    dest = pad_starts[e_flat] + rank
    blk_start = jnp.arange(N_BLOCKS, dtype=jnp.int32) * TB
    block_expert = jnp.minimum(jnp.searchsorted(pad_ends, blk_start, side='right'),
                               N_EXPERTS - 1).astype(jnp.int32)
    block_ok = (blk_start < pad_ends[-1]).astype(jnp.int32)
    src = jnp.zeros((N_BLOCKS * TB,), jnp.int32).at[dest].set(jnp.arange(N_ASSIGN, dtype=jnp.int32) // TOP_K)
    y_buf = _ffn(h[src], block_expert, block_ok, wg, wu, wd, layer)
    d2 = dest.reshape(M_ROWS, TOP_K)
    return y_buf[d2[:, 0]], y_buf[d2[:, 1]]


def _attn_kernel(*refs, tq, qb, hi, prompt):
    q_ref, k_ref, v_ref, km_ref, vm_ref, sink_ref = refs[:6]
    o_ref = refs[-1]
    _attn_body(q_ref, k_ref, v_ref, km_ref, vm_ref, sink_ref, o_ref, tq=tq, qb=qb, hi=hi, prompt=prompt)


def _attn_body(q_ref, k_ref, v_ref, km_ref, vm_ref, sink_ref, o_ref, *, tq, qb, hi, prompt):
    lane = lax.broadcasted_iota(jnp.int32, (KEYS, 128), 1)
    low = lane < HEAD_DIM
    key = lax.broadcasted_iota(jnp.int32, (1, 2 * KEYS), 1) % KEYS
    is_meta = jnp.logical_and(key >= BAND, key < BAND + N_META)
    zpad_k = jnp.zeros((KEYS - BAND - N_META, 128), BF16)
    ones_row = lax.broadcasted_iota(jnp.int32, (2 * KEYS, 128), 0) < KEYS
    ones_low = lax.broadcasted_iota(jnp.int32, (2 * KEYS, 128), 1) < HEAD_DIM
    sum_cols = jnp.where(ones_row == ones_low, 1.0, 0.0).astype(BF16)
    out_low = lax.broadcasted_iota(jnp.int32, (4 * tq, 128), 1) < HEAD_DIM

    def chunk(cc, carry):
        c_glob = pl.program_id(1) * qb + cc
        row0 = pl.multiple_of(cc * tq, tq)
        key0 = pl.multiple_of(c_glob * CHUNK, CHUNK) if prompt else 0
        lo = jnp.maximum(WIN_CHUNKS - c_glob, 0) * CHUNK if prompt else 0
        ok = jnp.logical_or(jnp.logical_and(key >= lo, key < hi), is_meta)
        bias = jnp.where(ok, 0.0, NEG_INF)
        for g in range(N_KV_HEADS):
            gl = slice(g * 128, (g + 1) * 128)
            k_all = jnp.concatenate([k_ref[0, pl.ds(key0, BAND), gl], km_ref[0, :, gl], zpad_k], axis=0)
            v_all = jnp.concatenate([v_ref[0, pl.ds(key0, BAND), gl], vm_ref[0, :, gl], zpad_k], axis=0)
            k2 = jnp.concatenate([jnp.where(low, k_all, 0), jnp.where(low, 0, k_all)], axis=0)
            v2 = jnp.concatenate([jnp.where(low, v_all, 0), jnp.where(low, 0, v_all)], axis=0)
            qp = jnp.concatenate(
                [q_ref[pl.ds(row0, tq), g * 512 + i * 128:g * 512 + (i + 1) * 128] for i in range(4)], axis=0)
            s = lax.dot_general(qp, k2, (((1,), (1,)), ((), ())), preferred_element_type=F32) + bias
            es, sink_e = [], []
            for p in range(2):
                sp = s[:, p * KEYS:(p + 1) * KEYS]
                sk = sink_ref[g, :, p:p + 1]
                m = jnp.maximum(jnp.max(sp, axis=-1, keepdims=True), sk)
                es.append(jnp.exp(sp - m).astype(BF16))
                sink_e.append(jnp.exp(sk - m))
            ox = _dot(jnp.concatenate(es, axis=1), jnp.concatenate([v2, sum_cols], axis=1))
            den = ox[:, 128:] + jnp.where(out_low, sink_e[0], sink_e[1])
            o = ox[:, :128] * (1.0 / den)
            for i in range(4):
                o_ref[pl.ds(row0, tq), g * 512 + i * 128:g * 512 + (i + 1) * 128] = (
                    o[i * tq:(i + 1) * tq, :].astype(BF16))
        return carry

    lax.fori_loop(0, qb, chunk, 0, unroll=min(4, qb))


def _attention(q, kd, vd, kmd, vmd, sinks, *, row0, n_b, n_rows, tq, qb, hi, prompt, dst=None):
    blk_rows = tq * qb
    n_i = n_rows // blk_rows
    blk0 = row0 // blk_rows
    per_b_meta = kmd.shape[0] > 1
    sk = sinks.reshape(N_KV_HEADS, 4, 1, 2)
    sk = jnp.broadcast_to(sk, (N_KV_HEADS, 4, tq, 2)).reshape(N_KV_HEADS, 4 * tq, 2)
    n_keys = kd.shape[1]
    kv_spec = pl.BlockSpec((1, n_keys, 512), lambda b, i: (b, 0, 0))
    meta_spec = pl.BlockSpec((1, N_META, 512), lambda b, i: (b if per_b_meta else 0, 0, 0))
    row_spec = pl.BlockSpec((blk_rows, D_MODEL), lambda b, i: (blk0 + b * n_i + i, 0))
    in_specs = [row_spec, kv_spec, kv_spec, meta_spec, meta_spec,
                pl.BlockSpec((N_KV_HEADS, 4 * tq, 2), lambda b, i: (0, 0, 0))]
    args = [q, kd, vd, kmd, vmd, sk]
    aliases = {}
    if dst is not None:
        aliases = {len(args): 0}
        in_specs.append(pl.BlockSpec(memory_space=pl.ANY))
        args.append(dst)
    return pl.pallas_call(
        functools.partial(_attn_kernel, tq=tq, qb=qb, hi=hi, prompt=prompt),
        grid=(n_b, n_i), in_specs=in_specs, out_specs=row_spec,
        out_shape=jax.ShapeDtypeStruct((M_ROWS, D_MODEL), BF16), input_output_aliases=aliases,
        compiler_params=_cparams(("parallel", "arbitrary")), name="swa")(*args)


def _cast_kernel(w_ref, o_ref):
    o_ref[...] = w_ref[...].astype(BF16)


def _cast_rows(w):
    n_l, n_e, d_hidden, d_out = w.shape
    spec = pl.BlockSpec((1, TF, d_out), lambda e, i: (e, i, 0))
    out = pl.pallas_call(
        _cast_kernel, grid=(n_l * n_e, d_hidden // TF), in_specs=[spec], out_specs=spec,
        out_shape=jax.ShapeDtypeStruct((n_l * n_e, d_hidden, d_out), BF16),
        compiler_params=_cparams(("parallel", "parallel")), name="cast")(w.reshape(n_l * n_e, d_hidden, d_out))
    return out.reshape(n_l, n_e, d_hidden, d_out)


def _tile_cols_kernel(w_ref, o_ref):
    for f in range(o_ref.shape[1]):
        o_ref[0, f] = w_ref[0, :, f * TF:(f + 1) * TF].astype(BF16)


def _tile_cols(w):
    n_l, n_e, d_in, d_hidden = w.shape
    n_f = d_hidden // TF
    out = pl.pallas_call(
        _tile_cols_kernel, grid=(n_l * n_e, d_in // CAST_ROWS),
        in_specs=[pl.BlockSpec((1, CAST_ROWS, d_hidden), lambda e, i: (e, i, 0))],
        out_specs=pl.BlockSpec((1, n_f, CAST_ROWS, TF), lambda e, i: (e, 0, i, 0)),
        out_shape=jax.ShapeDtypeStruct((n_l * n_e, n_f, d_in, TF), BF16),
        compiler_params=_cparams(("parallel", "parallel")), name="cast_tile")(w.reshape(n_l * n_e, d_in, d_hidden))
    return out.reshape(n_l, n_e, n_f, d_in, TF)


def _dup_heads(x):
    lead = x.shape[:-1]
    x = x.astype(BF16).reshape(*lead, N_KV_HEADS, 1, HEAD_DIM)
    return jnp.broadcast_to(x, (*lead, N_KV_HEADS, 2, HEAD_DIM)).reshape(*lead, 2 * KV_W)


def kernel(x_prompt, x_sample, state_conv, cache_meta_k, cache_meta_v, cache_win_k, cache_win_v, meta_tokens, g_conv, w_pw1, b_pw1, w_dw, b_dw, g_conv_ln, b_conv_ln, w_pw2, b_pw2, g_kv, w_kv, g_k, g_attn, w_q, g_q, sinks, w_o, g_ffn, w_ffn_gate, w_ffn_up, w_ffn_down, g_moe, w_router, w_moe_gate, w_moe_up, w_moe_down):
    bf = lambda w: w.astype(BF16)
    x = jnp.concatenate([x_prompt.reshape(N_PROMPT, D_MODEL), x_sample.reshape(N_SAMPLE, D_MODEL),
                         meta_tokens, jnp.zeros((M_ROWS - ROW_META - N_META, D_MODEL), F32)], axis=0)

    pos = jnp.concatenate([jnp.tile(N_META + jnp.arange(SEQ), BATCH),
                           jnp.tile(N_META + PAST_LEN + jnp.arange(DEC_SEQ), DEC_BATCH),
                           jnp.arange(N_META), jnp.zeros((M_ROWS - ROW_META - N_META,), jnp.int32)])
    half = HEAD_DIM // 2
    inv = jnp.exp(jnp.arange(half, dtype=F32) * (-2.0 * math.log(ROPE_THETA) / HEAD_DIM))
    ang = pos.astype(F32)[:, None] * inv[None, :]
    cos_t = jnp.tile(jnp.concatenate([jnp.cos(ang), jnp.cos(ang)], axis=-1), (1, SLAB // HEAD_DIM))
    sin_t = jnp.tile(jnp.concatenate([-jnp.sin(ang), jnp.sin(ang)], axis=-1), (1, SLAB // HEAD_DIM))
    hd = jnp.arange(SLAB) // HEAD_DIM
    bd = (hd[:, None] == hd[None, :]).astype(BF16)

    dense_expert = jnp.zeros((M_ROWS // TB,), jnp.int32)
    dense_ok = jnp.ones((M_ROWS // TB,), jnp.int32)
    conv_p, conv_s = [], []
    w_pw1, w_pw2, w_q, w_o, w_kv = bf(w_pw1), bf(w_pw2), bf(w_q), bf(w_o), bf(w_kv)
    w_ffn_gate, w_ffn_up = _tile_cols(w_ffn_gate[:, None]), _tile_cols(w_ffn_up[:, None])
    w_moe_gate, w_moe_up = _tile_cols(w_moe_gate), _tile_cols(w_moe_up)
    w_ffn_down, w_moe_down = _cast_rows(w_ffn_down[:, None]), _cast_rows(w_moe_down)

    (h,) = _norm(x, gains=(g_conv[0],))
    for l in range(N_A):
        u = _glu(h, w_pw1, b_pw1[l], l)
        u_meta = u[ROW_META:ROW_META + N_META]
        u_samp = u[ROW_SAMPLE:ROW_SAMPLE + N_SAMPLE].reshape(DEC_BATCH, DEC_SEQ, D_MODEL)
        hist_p = jnp.broadcast_to(
            jnp.concatenate([jnp.zeros((HIST - N_META, D_MODEL), F32), u_meta], axis=0)[None],
            (BATCH, HIST, D_MODEL))
        hist_s = jnp.zeros((N_SMALL // DEC_SEQ, HIST, D_MODEL), F32).at[:DEC_BATCH, HIST - (CONV_W - 1):].set(
            state_conv[l])
        conv_args = (w_dw[l], b_dw[l], g_conv_ln[l], b_conv_ln[l])
        c = _conv(u, hist_p, *conv_args, row0=0, n_seq=BATCH, seq_len=SEQ, tt=CONV_TT)
        c = _conv(u, hist_s, *conv_args, row0=ROW_SAMPLE, n_seq=N_SMALL // DEC_SEQ, seq_len=DEC_SEQ,
                  tt=DEC_SEQ, dst=c)
        conv_p.append(jnp.stack([u[(b + 1) * SEQ - (CONV_W - 1):(b + 1) * SEQ] for b in range(BATCH)]))
        conv_s.append(jnp.concatenate([state_conv[l], u_samp], axis=1)[:, -(CONV_W - 1):])
        if l == 0:
            x, h = _resid_mm(c, w_pw2, b_pw2[l], x, l, g_ffn[0])
            y = _ffn(h, dense_expert, dense_ok, w_ffn_gate, w_ffn_up, w_ffn_down, 0)
            x, h = _norm(x, ys=(y,), gains=(g_conv[1],))
        else:
            x, h, idx, gates = _resid_mm(c, w_pw2, b_pw2[l], x, l, g_moe[0], w_router[0])
            y0, y1 = _moe(h, idx, w_moe_gate, w_moe_up, w_moe_down, 0)
            x, h_kv, h = _norm(x, ys=(y0, y1), gates=gates, gains=(g_kv, g_attn[0]))

    k, v = _shared_kv(h_kv, w_kv, g_k, cos_t, sin_t, bd)
    k_fr = k[:N_PROMPT].reshape(BATCH, SEQ, KV_W)
    v_fr = v[:N_PROMPT].reshape(BATCH, SEQ, KV_W)
    k_new = k[ROW_SAMPLE:ROW_SAMPLE + N_SAMPLE].reshape(DEC_BATCH, DEC_SEQ, KV_W)
    v_new = v[ROW_SAMPLE:ROW_SAMPLE + N_SAMPLE].reshape(DEC_BATCH, DEC_SEQ, KV_W)
    k_meta_p = k[ROW_META:ROW_META + N_META]
    v_meta_p = v[ROW_META:ROW_META + N_META]
    front = jnp.zeros((BATCH, WINDOW, KV_W), F32)
    kd_p = _dup_heads(jnp.concatenate([front, k_fr], axis=1))
    vd_p = _dup_heads(jnp.concatenate([front, v_fr], axis=1))
    tail = jnp.zeros((DEC_BATCH, BAND - WINDOW - DEC_SEQ, KV_W), F32)
    pad_b = lambda t: jnp.concatenate(
        [t, jnp.zeros((N_SMALL // DEC_SEQ - DEC_BATCH, *t.shape[1:]), t.dtype)], axis=0)
    kd_s = _dup_heads(pad_b(jnp.concatenate([cache_win_k.reshape(DEC_BATCH, WINDOW, KV_W), k_new, tail], axis=1)))
    vd_s = _dup_heads(pad_b(jnp.concatenate([cache_win_v.reshape(DEC_BATCH, WINDOW, KV_W), v_new, tail], axis=1)))
    kmd_p, vmd_p = _dup_heads(k_meta_p)[None], _dup_heads(v_meta_p)[None]
    kmd_s = _dup_heads(pad_b(cache_meta_k.reshape(DEC_BATCH, N_META, KV_W)))
    vmd_s = _dup_heads(pad_b(cache_meta_v.reshape(DEC_BATCH, N_META, KV_W)))

    for j in range(DEPTH - N_A):
        q = _queries(h, w_q, g_q[j], cos_t, sin_t, bd, j)
        o = _attention(q, kd_p, vd_p, kmd_p, vmd_p, sinks[j], row0=0, n_b=BATCH, n_rows=SEQ, tq=CHUNK,
                       qb=ATT_QB, hi=BAND, prompt=True)
        o = _attention(q, kd_s, vd_s, kmd_s, vmd_s, sinks[j], row0=ROW_SAMPLE, n_b=N_SMALL // DEC_SEQ,
                       n_rows=DEC_SEQ, tq=DEC_SEQ, qb=1, hi=WINDOW + DEC_SEQ, prompt=False, dst=o)
        no_bias = jnp.zeros((D_MODEL,), F32)
        if j == 0:
            x, h = _resid_mm(o, w_o, no_bias, x, j, g_ffn[1])
            y = _ffn(h, dense_expert, dense_ok, w_ffn_gate, w_ffn_up, w_ffn_down, 1)
            x, h = _norm(x, ys=(y,), gains=(g_attn[1],))
        else:
            x, h, idx, gates = _resid_mm(o, w_o, no_bias, x, j, g_moe[1], w_router[1])
            y0, y1 = _moe(h, idx, w_moe_gate, w_moe_up, w_moe_down, 1)
            (y_prompt,) = _norm(x, ys=(y0, y1), gates=gates, tr=256, row0=0, n_rows=N_PROMPT)
            (y_sample,) = _norm(x, ys=(y0, y1), gates=gates, tr=N_SAMPLE, row0=ROW_SAMPLE, n_rows=N_SAMPLE)

    y_prompt = y_prompt.reshape(BATCH, SEQ, D_MODEL)
    y_sample = y_sample.reshape(DEC_BATCH, DEC_SEQ, D_MODEL)
    kv4 = lambda t: t.reshape(*t.shape[:-1], N_KV_HEADS, HEAD_DIM)
    meta_k_p = jnp.broadcast_to(kv4(k_meta_p)[None], (BATCH, N_META, N_KV_HEADS, HEAD_DIM))
    meta_v_p = jnp.broadcast_to(kv4(v_meta_p)[None], (BATCH, N_META, N_KV_HEADS, HEAD_DIM))
    win_k_s = jnp.concatenate([cache_win_k, kv4(k_new)], axis=1)[:, -WINDOW:]
    win_v_s = jnp.concatenate([cache_win_v, kv4(v_new)], axis=1)[:, -WINDOW:]
    return (y_prompt, y_sample, jnp.stack(conv_p), jnp.stack(conv_s), meta_k_p, meta_v_p,
            kv4(k_fr[:, -WINDOW:]), kv4(v_fr[:, -WINDOW:]), win_k_s, win_v_s)
```

```python
import functools
import math

import jax
import jax.numpy as jnp
from jax import lax
from jax.experimental import pallas as pl
from jax.experimental.pallas import tpu as pltpu

D_MODEL = 2048
BATCH = 8
SEQ = 4096
DEPTH = 4
DEC_BATCH = 8
DEC_SEQ = 16
PAST_LEN = 4096
CHUNK = 64
N_META = 16
N_A = DEPTH // 2
CONV_W = 31
HEAD_DIM = 64
N_HEADS = D_MODEL // HEAD_DIM
N_KV_HEADS = N_HEADS // 8
GROUP = N_HEADS // N_KV_HEADS
WINDOW = 128
WIN_CHUNKS = WINDOW // CHUNK
ROPE_THETA = 10000.0
D_FF = 5632
N_EXPERTS = 8
TOP_K = 2
D_EXPERT = 7168
EPS = 1e-6
NEG_INF = -1e30

F32 = jnp.float32
BF16 = jnp.bfloat16

N_PROMPT = BATCH * SEQ
N_SAMPLE = DEC_BATCH * DEC_SEQ
N_SMALL = 256
M_ROWS = N_PROMPT + N_SMALL
ROW_SAMPLE = N_PROMPT
ROW_META = N_PROMPT + N_SAMPLE

TM = 768
TN = 1024
TF = 512
TB = TM
N_ASSIGN = M_ROWS * TOP_K
N_BLOCKS = N_ASSIGN // TB + N_EXPERTS
TR = 384
SLAB = 256
HIST = 32
CONV_TT = 128
CONV_RG = 128
CONV_LS = 128
FFN_SUBS = (384, 384)
TMR = 384
R_SUB = 192
Q_SUB = 256
CAST_ROWS = 256
KV_W = N_KV_HEADS * HEAD_DIM
KEYS = 256
BAND = (WIN_CHUNKS + 1) * CHUNK
ATT_QB = 8
VMEM_LIMIT = 56 * 1024 * 1024


def _cparams(sem):
    return pltpu.CompilerParams(dimension_semantics=sem, vmem_limit_bytes=VMEM_LIMIT)


def _dot(a, b):
    return jnp.dot(a, b, preferred_element_type=F32)


def _norm_kernel(*refs, n_y, gated, n_g, write_x):
    x_ref = refs[0]
    y_refs = refs[1:1 + n_y]
    pos = 1 + n_y
    gate_ref = refs[pos] if gated else None
    pos += 1 if gated else 0
    g_refs = refs[pos:pos + n_g]
    pos += n_g
    outs = refs[pos:]
    x = x_ref[...]
    for i, y_ref in enumerate(y_refs):
        y = y_ref[...]
        x = x + (gate_ref[:, i:i + 1] * y if gated else y)
    o = 0
    if write_x:
        outs[0][...] = x
        o = 1
    if n_g:
        xn = x * lax.rsqrt(jnp.mean(x * x, axis=-1, keepdims=True) + EPS)
        for i, g_ref in enumerate(g_refs):
            outs[o + i][...] = (xn * g_ref[...]).astype(BF16)


def _norm(x, ys=(), gates=None, gains=(), *, tr=TR, row0=0, n_rows=M_ROWS):
    n_y, n_g = len(ys), len(gains)
    write_x = n_y > 0
    blk0 = row0 // tr
    row_in = pl.BlockSpec((tr, D_MODEL), lambda i: (blk0 + i, 0))
    row_out = pl.BlockSpec((tr, D_MODEL), lambda i: (i, 0))
    in_specs = [row_in] * (1 + n_y)
    args = [x, *ys]
    if gates is not None:
        in_specs.append(pl.BlockSpec((tr, 128), lambda i: (blk0 + i, 0)))
        args.append(gates)
    for g in gains:
        in_specs.append(pl.BlockSpec((1, D_MODEL), lambda i: (0, 0)))
        args.append(g.reshape(1, D_MODEL))
    out_shape, out_specs = [], []
    if write_x:
        out_shape.append(jax.ShapeDtypeStruct((n_rows, D_MODEL), F32))
        out_specs.append(row_out)
    for _ in gains:
        out_shape.append(jax.ShapeDtypeStruct((n_rows, D_MODEL), BF16))
        out_specs.append(row_out)
    return pl.pallas_call(
        functools.partial(_norm_kernel, n_y=n_y, gated=gates is not None, n_g=n_g, write_x=write_x),
        grid=(n_rows // tr,), in_specs=in_specs, out_specs=out_specs, out_shape=out_shape,
        compiler_params=_cparams(("parallel",)), name="norm")(*args)


def _glu_kernel(h_ref, wa_ref, wb_ref, ba_ref, bb_ref, u_ref):
    h = h_ref[...]
    a = _dot(h, wa_ref[0]) + ba_ref[...]
    b = _dot(h, wb_ref[0]) + bb_ref[...]
    u_ref[...] = a * jax.nn.sigmoid(b)


def _glu(h, w, b, layer):
    nj = D_MODEL // TN
    b2 = b.reshape(1, 2 * D_MODEL)
    return pl.pallas_call(
        _glu_kernel, grid=(nj, M_ROWS // TM),
        in_specs=[pl.BlockSpec((TM, D_MODEL), lambda j, i: (i, 0)),
                  pl.BlockSpec((1, D_MODEL, TN), lambda j, i: (layer, 0, j)),
                  pl.BlockSpec((1, D_MODEL, TN), lambda j, i: (layer, 0, j + nj)),
                  pl.BlockSpec((1, TN), lambda j, i: (0, j)),
                  pl.BlockSpec((1, TN), lambda j, i: (0, j + nj))],
        out_specs=pl.BlockSpec((TM, TN), lambda j, i: (i, j)),
        out_shape=jax.ShapeDtypeStruct((M_ROWS, D_MODEL), F32),
        compiler_params=_cparams(("parallel", "parallel")), name="pw1_glu")(h, w, w, b2, b2)


def _top2(h, wr_ref, idx_ref, gate_ref, rows):
    h_hi = h.astype(BF16)
    h_lo = (h - h_hi.astype(F32)).astype(BF16)
    logits = _dot(h_hi, wr_ref[0]) + (_dot(h_hi, wr_ref[1]) + _dot(h_lo, wr_ref[0]))
    lane_i = lax.broadcasted_iota(jnp.int32, logits.shape, 1)
    lane = lane_i.astype(F32)
    logits = jnp.where(lane_i < N_EXPERTS, logits, -jnp.inf)
    m1 = jnp.max(logits, axis=-1, keepdims=True)
    i1 = jnp.min(jnp.where(logits == m1, lane, 128.0), axis=-1, keepdims=True)
    rest = jnp.where(lane == i1, -jnp.inf, logits)
    m2 = jnp.max(rest, axis=-1, keepdims=True)
    i2 = jnp.min(jnp.where(rest == m2, lane, 128.0), axis=-1, keepdims=True)
    e2 = jnp.exp(m2 - m1)
    g1 = 1.0 / (1.0 + e2)
    idx_ref[rows, :] = jnp.where(lane_i == 0, i1, i2).astype(jnp.int32)
    gate_ref[rows, :] = jnp.where(lane_i == 0, g1, e2 * g1)


def _resid_kernel(*refs, routed):
    if routed:
        x_ref, w_ref, b_ref, r_ref, g_ref, wr_ref, xo_ref, h_ref, idx_ref, gate_ref = refs
    else:
        x_ref, w_ref, b_ref, r_ref, g_ref, xo_ref, h_ref = refs
    for s in range(TMR // R_SUB):
        rows = slice(s * R_SUB, (s + 1) * R_SUB)
        xn = r_ref[rows, :] + (_dot(x_ref[rows, :], w_ref[0]) + b_ref[...])
        xo_ref[rows, :] = xn
        h = xn * lax.rsqrt(jnp.mean(xn * xn, axis=-1, keepdims=True) + EPS) * g_ref[...]
        h_ref[rows, :] = h.astype(BF16)
        if routed:
            _top2(h, wr_ref, idx_ref, gate_ref, rows)


def _resid_mm(x, w, b, r, layer, gain, w_router=None):
    routed = w_router is not None
    row = lambda width: pl.BlockSpec((TMR, width), lambda i: (i, 0))
    vec = pl.BlockSpec((1, D_MODEL), lambda i: (0, 0))
    in_specs = [row(D_MODEL), pl.BlockSpec((1, D_MODEL, D_MODEL), lambda i: (layer, 0, 0)), vec, row(D_MODEL), vec]
    args = [x, w, b.reshape(1, D_MODEL), r, gain.reshape(1, D_MODEL)]
    out_specs = [row(D_MODEL), row(D_MODEL)]
    out_shape = [jax.ShapeDtypeStruct((M_ROWS, D_MODEL), F32), jax.ShapeDtypeStruct((M_ROWS, D_MODEL), BF16)]
    if routed:
        in_specs.append(pl.BlockSpec((2, D_MODEL, 128), lambda i: (0, 0, 0)))
        wr = jnp.zeros((D_MODEL, 128), F32).at[:, :N_EXPERTS].set(w_router)
        wr_hi = wr.astype(BF16)
        args.append(jnp.stack([wr_hi, (wr - wr_hi.astype(F32)).astype(BF16)]))
        out_specs += [row(128), row(128)]
        out_shape += [jax.ShapeDtypeStruct((M_ROWS, 128), jnp.int32), jax.ShapeDtypeStruct((M_ROWS, 128), F32)]
    return pl.pallas_call(
        functools.partial(_resid_kernel, routed=routed), grid=(M_ROWS // TMR,),
        in_specs=in_specs, out_specs=out_specs, out_shape=out_shape,
        compiler_params=_cparams(("parallel",)), name="proj_resid_norm")(*args)


def _head_norm_rope(x, gain, cos, sin, bd):
    ssq = _dot((x * x).astype(BF16), bd)
    xn = x * lax.rsqrt(ssq * (1.0 / HEAD_DIM) + EPS) * gain
    lane = lax.broadcasted_iota(jnp.int32, xn.shape, 1)
    first_half = (lane % HEAD_DIM) < (HEAD_DIM // 2)
    rot = jnp.where(first_half, pltpu.roll(xn, SLAB - HEAD_DIM // 2, 1), pltpu.roll(xn, HEAD_DIM // 2, 1))
    return xn * cos + rot * sin


def _q_kernel(h_ref, w_ref, g_ref, cos_ref, sin_ref, bd_ref, q_ref):
    bd, gain = bd_ref[...], g_ref[...]
    scale = HEAD_DIM ** -0.5
    for r in range(TM // Q_SUB):
        rows = slice(r * Q_SUB, (r + 1) * Q_SUB)
        acc = _dot(h_ref[rows, :], w_ref[0])
        cos, sin = cos_ref[rows, :], sin_ref[rows, :]
        for s in range(TN // SLAB):
            q = _head_norm_rope(acc[:, s * SLAB:(s + 1) * SLAB], gain, cos, sin, bd)
            q_ref[rows, s * SLAB:(s + 1) * SLAB] = (q * scale).astype(BF16)


def _queries(h, w, g_q, cos, sin, bd, layer):
    return pl.pallas_call(
        _q_kernel, grid=(D_MODEL // TN, M_ROWS // TM),
        in_specs=[pl.BlockSpec((TM, D_MODEL), lambda j, i: (i, 0)),
                  pl.BlockSpec((1, D_MODEL, TN), lambda j, i: (layer, 0, j)),
                  pl.BlockSpec((1, SLAB), lambda j, i: (0, 0)),
                  pl.BlockSpec((TM, SLAB), lambda j, i: (i, 0)),
                  pl.BlockSpec((TM, SLAB), lambda j, i: (i, 0)),
                  pl.BlockSpec((SLAB, SLAB), lambda j, i: (0, 0))],
        out_specs=pl.BlockSpec((TM, TN), lambda j, i: (i, j)),
        out_shape=jax.ShapeDtypeStruct((M_ROWS, D_MODEL), BF16),
        compiler_params=_cparams(("parallel", "parallel")), name="q_proj")(
            h, w, jnp.tile(g_q, SLAB // HEAD_DIM).reshape(1, SLAB), cos, sin, bd)


def _kv_kernel(h_ref, w_ref, g_ref, cos_ref, sin_ref, bd_ref, k_ref, v_ref):
    acc = _dot(h_ref[...], w_ref[...])
    k_ref[...] = _head_norm_rope(acc[:, :KV_W], g_ref[...], cos_ref[...], sin_ref[...], bd_ref[...])
    v_ref[...] = acc[:, KV_W:]


def _shared_kv(h, w, g_k, cos, sin, bd):
    return pl.pallas_call(
        _kv_kernel, grid=(M_ROWS // TM,),
        in_specs=[pl.BlockSpec((TM, D_MODEL), lambda i: (i, 0)),
                  pl.BlockSpec((D_MODEL, 2 * KV_W), lambda i: (0, 0)),
                  pl.BlockSpec((1, SLAB), lambda i: (0, 0)),
                  pl.BlockSpec((TM, SLAB), lambda i: (i, 0)),
                  pl.BlockSpec((TM, SLAB), lambda i: (i, 0)),
                  pl.BlockSpec((SLAB, SLAB), lambda i: (0, 0))],
        out_specs=[pl.BlockSpec((TM, KV_W), lambda i: (i, 0))] * 2,
        out_shape=[jax.ShapeDtypeStruct((M_ROWS, KV_W), F32)] * 2,
        compiler_params=_cparams(("parallel",)), name="kv_proj")(
            h, w, jnp.tile(g_k, SLAB // HEAD_DIM).reshape(1, SLAB), cos, sin, bd)


def _conv_kernel(*refs, tt, use_prev, has_dst):
    c_ref, ext_ref, acc_ref = refs[-3:]
    ins = refs[:-4] if has_dst else refs[:-3]
    if use_prev:
        u_ref, prev_ref, hist_ref, w_ref, bdw_ref, gln_ref, bln_ref = ins
    else:
        u_ref, hist_ref, w_ref, bdw_ref, gln_ref, bln_ref = ins
    if use_prev:
        first = pl.program_id(1) == 0

        @pl.when(first)
        def _():
            ext_ref[0:HIST, :] = hist_ref[0]

        @pl.when(jnp.logical_not(first))
        def _():
            ext_ref[0:HIST, :] = prev_ref[...]
    else:
        ext_ref[0:HIST, :] = hist_ref[0]
    ext_ref[HIST:HIST + tt, :] = u_ref[...]
    rg = min(CONV_RG, tt)
    off = HIST - (CONV_W - 1)
    for r in range(tt // rg):
        for s in range(D_MODEL // CONV_LS):
            ls = slice(s * CONV_LS, (s + 1) * CONV_LS)
            acc = jnp.zeros((rg, CONV_LS), F32)
            for res in range(8):
                offs = [o for o in range(off, off + CONV_W) if o % 8 == res]
                win = ext_ref[r * rg + offs[0]:r * rg + offs[-1] + rg, ls]
                part = None
                for o in offs:
                    k = o - off
                    term = win[o - offs[0]:o - offs[0] + rg, :] * w_ref[k:k + 1, ls]
                    part = term if part is None else part + term
                acc = acc + part
            acc_ref[r * rg:(r + 1) * rg, ls] = acc
    x = acc_ref[...] + bdw_ref[...]
    xc = x - jnp.mean(x, axis=-1, keepdims=True)
    y = xc * lax.rsqrt(jnp.mean(xc * xc, axis=-1, keepdims=True) + EPS)
    y = y * gln_ref[...] + bln_ref[...]
    c_ref[...] = (y * jax.nn.sigmoid(y)).astype(BF16)


def _conv(u, hist, w_dw, b_dw, g_ln, b_ln, *, row0, n_seq, seq_len, tt, dst=None):
    n_t = seq_len // tt
    use_prev = n_t > 1
    blk0 = row0 // tt
    wpad = jnp.zeros((32, D_MODEL), F32).at[:CONV_W].set(w_dw)
    vec = pl.BlockSpec((1, D_MODEL), lambda s, i: (0, 0))
    in_specs = [pl.BlockSpec((tt, D_MODEL), lambda s, i: (blk0 + s * n_t + i, 0))]
    args = [u]
    if use_prev:
        per = tt // HIST
        in_specs.append(pl.BlockSpec(
            (HIST, D_MODEL), lambda s, i: (jnp.maximum((blk0 + s * n_t + i) * per - 1, 0), 0)))
        args.append(u)
    in_specs += [pl.BlockSpec((1, HIST, D_MODEL), lambda s, i: (s, 0, 0)),
                 pl.BlockSpec((32, D_MODEL), lambda s, i: (0, 0)), vec, vec, vec]
    args += [hist, wpad, b_dw.reshape(1, D_MODEL), g_ln.reshape(1, D_MODEL), b_ln.reshape(1, D_MODEL)]
    aliases = {}
    if dst is not None:
        aliases = {len(args): 0}
        in_specs.append(pl.BlockSpec(memory_space=pl.ANY))
        args.append(dst)
    return pl.pallas_call(
        functools.partial(_conv_kernel, tt=tt, use_prev=use_prev, has_dst=dst is not None), grid=(n_seq, n_t),
        in_specs=in_specs,
        out_specs=pl.BlockSpec((tt, D_MODEL), lambda s, i: (blk0 + s * n_t + i, 0)),
        out_shape=jax.ShapeDtypeStruct((M_ROWS, D_MODEL), BF16),
        input_output_aliases=aliases,
        scratch_shapes=[pltpu.VMEM((HIST + tt, D_MODEL), F32), pltpu.VMEM((tt, D_MODEL), F32)],
        compiler_params=_cparams(("parallel", "arbitrary")), name="conv_ln_swish")(*args)


def _ffn_kernel(be_ref, nv_ref, x_ref, wg_ref, wu_ref, wd_ref, o_ref, acc_ref):
    i, f = pl.program_id(0), pl.program_id(1)
    n_valid = nv_ref[i]

    @pl.when(f == 0)
    def _():
        acc_ref[...] = jnp.zeros_like(acc_ref)

    def run(subs):
        r0 = 0
        for n in subs:
            rows = slice(r0, r0 + n)
            r0 += n
            x = x_ref[rows, :]
            g = _dot(x, wg_ref[0, 0, 0])
            u = _dot(x, wu_ref[0, 0, 0])
            a = (g * jax.nn.sigmoid(g) * u).astype(BF16)
            acc_ref[rows, :] += _dot(a, wd_ref[0, 0])

    @pl.when(n_valid > FFN_SUBS[0])
    def _():
        run(FFN_SUBS)

    @pl.when(jnp.logical_and(n_valid > 0, n_valid <= FFN_SUBS[0]))
    def _():
        run(FFN_SUBS[:1])

    @pl.when(f == pl.num_programs(1) - 1)
    def _():
        o_ref[...] = acc_ref[...].astype(BF16)


def _ffn(x, block_expert, block_rows, wg, wu, wd, layer):
    n_blocks = x.shape[0] // TB
    n_f = wg.shape[2]

    def f_idx(i, f, ok):
        return jnp.where(ok[i] > 0, f, n_f - 1)

    grid_spec = pltpu.PrefetchScalarGridSpec(
        num_scalar_prefetch=2, grid=(n_blocks, n_f),
        in_specs=[pl.BlockSpec((TB, D_MODEL), lambda i, f, be, ok: (i, 0)),
                  pl.BlockSpec((1, 1, 1, D_MODEL, TF),
                               lambda i, f, be, ok: (layer, be[i], f_idx(i, f, ok), 0, 0)),
                  pl.BlockSpec((1, 1, 1, D_MODEL, TF),
                               lambda i, f, be, ok: (layer, be[i], f_idx(i, f, ok), 0, 0)),
                  pl.BlockSpec((1, 1, TF, D_MODEL), lambda i, f, be, ok: (layer, be[i], f_idx(i, f, ok), 0))],
        out_specs=pl.BlockSpec((TB, D_MODEL), lambda i, f, be, ok: (i, 0)),
        scratch_shapes=[pltpu.VMEM((TB, D_MODEL), F32)])
    return pl.pallas_call(
        _ffn_kernel, grid_spec=grid_spec,
        out_shape=jax.ShapeDtypeStruct((n_blocks * TB, D_MODEL), BF16),
        compiler_params=_cparams(("parallel", "arbitrary")), name="swiglu")(
            block_expert, block_rows, x, wg, wu, wd)


def _moe(h, idx, wg, wu, wd, layer):
    e_flat = idx[:, :TOP_K].reshape(N_ASSIGN)
    onehot = (e_flat[:, None] == jnp.arange(N_EXPERTS, dtype=jnp.int32)[None, :]).astype(jnp.int32)
    csum = jnp.cumsum(onehot, axis=0)
    rank = jnp.sum(csum * onehot, axis=-1) - 1
    counts = csum[-1]
    padded = (counts + TB - 1) // TB * TB
    pad_ends = jnp.cumsum(padded)
    pad_starts = pad_ends - padded
    dest = pad_starts[e_flat] + rank
    blk_start = jnp.arange(N_BLOCKS, dtype=jnp.int32) * TB
    block_expert = jnp.minimum(jnp.searchsorted(pad_ends, blk_start, side='right'),
                               N_EXPERTS - 1).astype(jnp.int32)
    real_end = pad_starts + counts
    block_rows = jnp.where(blk_start < pad_ends[-1],
                           jnp.clip(real_end[block_expert] - blk_start, 0, TB), 0).astype(jnp.int32)
    src = jnp.zeros((N_BLOCKS * TB,), jnp.int32).at[dest].set(jnp.arange(N_ASSIGN, dtype=jnp.int32) // TOP_K)
    y_buf = _ffn(h[src], block_expert, block_rows, wg, wu, wd, layer)
    d2 = dest.reshape(M_ROWS, TOP_K)
    return y_buf[d2[:, 0]], y_buf[d2[:, 1]]


def _attn_kernel(*refs, tq, qb, hi, prompt):
    q_ref, k_ref, v_ref, km_ref, vm_ref, sink_ref = refs[:6]
    o_ref = refs[-1]
    _attn_body(q_ref, k_ref, v_ref, km_ref, vm_ref, sink_ref, o_ref, tq=tq, qb=qb, hi=hi, prompt=prompt)


def _attn_body(q_ref, k_ref, v_ref, km_ref, vm_ref, sink_ref, o_ref, *, tq, qb, hi, prompt):
    lane = lax.broadcasted_iota(jnp.int32, (KEYS, 128), 1)
    low = lane < HEAD_DIM
    key = lax.broadcasted_iota(jnp.int32, (1, 2 * KEYS), 1) % KEYS
    is_meta = jnp.logical_and(key >= BAND, key < BAND + N_META)
    zpad_k = jnp.zeros((KEYS - BAND - N_META, 128), BF16)
    ones_row = lax.broadcasted_iota(jnp.int32, (2 * KEYS, 128), 0) < KEYS
    ones_low = lax.broadcasted_iota(jnp.int32, (2 * KEYS, 128), 1) < HEAD_DIM
    sum_cols = jnp.where(ones_row == ones_low, 1.0, 0.0).astype(BF16)
    out_low = lax.broadcasted_iota(jnp.int32, (4 * tq, 128), 1) < HEAD_DIM

    def chunk(cc, carry):
        c_glob = pl.program_id(1) * qb + cc
        row0 = pl.multiple_of(cc * tq, tq)
        key0 = pl.multiple_of(c_glob * CHUNK, CHUNK) if prompt else 0
        lo = jnp.maximum(WIN_CHUNKS - c_glob, 0) * CHUNK if prompt else 0
        ok = jnp.logical_or(jnp.logical_and(key >= lo, key < hi), is_meta)
        bias = jnp.where(ok, 0.0, NEG_INF)
        for g in range(N_KV_HEADS):
            gl = slice(g * 128, (g + 1) * 128)
            k_all = jnp.concatenate([k_ref[0, pl.ds(key0, BAND), gl], km_ref[0, :, gl], zpad_k], axis=0)
            v_all = jnp.concatenate([v_ref[0, pl.ds(key0, BAND), gl], vm_ref[0, :, gl], zpad_k], axis=0)
            k2 = jnp.concatenate([jnp.where(low, k_all, 0), jnp.where(low, 0, k_all)], axis=0)
            v2 = jnp.concatenate([jnp.where(low, v_all, 0), jnp.where(low, 0, v_all)], axis=0)
            qp = jnp.concatenate(
                [q_ref[pl.ds(row0, tq), g * 512 + i * 128:g * 512 + (i + 1) * 128] for i in range(4)], axis=0)
            s = lax.dot_general(qp, k2, (((1,), (1,)), ((), ())), preferred_element_type=F32) + bias
            es, sink_e = [], []
            for p in range(2):
                sp = s[:, p * KEYS:(p + 1) * KEYS]
                sk = sink_ref[g, :, p:p + 1]
                m = jnp.maximum(jnp.max(sp, axis=-1, keepdims=True), sk)
                es.append(jnp.exp(sp - m).astype(BF16))
                sink_e.append(jnp.exp(sk - m))
            ox = _dot(jnp.concatenate(es, axis=1), jnp.concatenate([v2, sum_cols], axis=1))
            den = ox[:, 128:] + jnp.where(out_low, sink_e[0], sink_e[1])
            o = ox[:, :128] * (1.0 / den)
            for i in range(4):
                o_ref[pl.ds(row0, tq), g * 512 + i * 128:g * 512 + (i + 1) * 128] = (
                    o[i * tq:(i + 1) * tq, :].astype(BF16))
        return carry

    lax.fori_loop(0, qb, chunk, 0, unroll=min(4, qb))


def _attention(q, kd, vd, kmd, vmd, sinks, *, row0, n_b, n_rows, tq, qb, hi, prompt, dst=None):
    blk_rows = tq * qb
    n_i = n_rows // blk_rows
    blk0 = row0 // blk_rows
    per_b_meta = kmd.shape[0] > 1
    sk = sinks.reshape(N_KV_HEADS, 4, 1, 2)
    sk = jnp.broadcast_to(sk, (N_KV_HEADS, 4, tq, 2)).reshape(N_KV_HEADS, 4 * tq, 2)
    n_keys = kd.shape[1]
    kv_spec = pl.BlockSpec((1, n_keys, 512), lambda b, i: (b, 0, 0))
    meta_spec = pl.BlockSpec((1, N_META, 512), lambda b, i: (b if per_b_meta else 0, 0, 0))
    row_spec = pl.BlockSpec((blk_rows, D_MODEL), lambda b, i: (blk0 + b * n_i + i, 0))
    in_specs = [row_spec, kv_spec, kv_spec, meta_spec, meta_spec,
                pl.BlockSpec((N_KV_HEADS, 4 * tq, 2), lambda b, i: (0, 0, 0))]
    args = [q, kd, vd, kmd, vmd, sk]
    aliases = {}
    if dst is not None:
        aliases = {len(args): 0}
        in_specs.append(pl.BlockSpec(memory_space=pl.ANY))
        args.append(dst)
    return pl.pallas_call(
        functools.partial(_attn_kernel, tq=tq, qb=qb, hi=hi, prompt=prompt),
        grid=(n_b, n_i), in_specs=in_specs, out_specs=row_spec,
        out_shape=jax.ShapeDtypeStruct((M_ROWS, D_MODEL), BF16), input_output_aliases=aliases,
        compiler_params=_cparams(("parallel", "arbitrary")), name="swa")(*args)


def _cast_kernel(w_ref, o_ref):
    o_ref[...] = w_ref[...].astype(BF16)


def _cast_rows(w):
    n_l, n_e, d_hidden, d_out = w.shape
    spec = pl.BlockSpec((1, TF, d_out), lambda e, i: (e, i, 0))
    out = pl.pallas_call(
        _cast_kernel, grid=(n_l * n_e, d_hidden // TF), in_specs=[spec], out_specs=spec,
        out_shape=jax.ShapeDtypeStruct((n_l * n_e, d_hidden, d_out), BF16),
        compiler_params=_cparams(("parallel", "parallel")), name="cast")(w.reshape(n_l * n_e, d_hidden, d_out))
    return out.reshape(n_l, n_e, d_hidden, d_out)


def _tile_cols_kernel(w_ref, o_ref):
    for f in range(o_ref.shape[1]):
        o_ref[0, f] = w_ref[0, :, f * TF:(f + 1) * TF].astype(BF16)


def _tile_cols(w):
    n_l, n_e, d_in, d_hidden = w.shape
    n_f = d_hidden // TF
    out = pl.pallas_call(
        _tile_cols_kernel, grid=(n_l * n_e, d_in // CAST_ROWS),
        in_specs=[pl.BlockSpec((1, CAST_ROWS, d_hidden), lambda e, i: (e, i, 0))],
        out_specs=pl.BlockSpec((1, n_f, CAST_ROWS, TF), lambda e, i: (e, 0, i, 0)),
        out_shape=jax.ShapeDtypeStruct((n_l * n_e, n_f, d_in, TF), BF16),
        compiler_params=_cparams(("parallel", "parallel")), name="cast_tile")(w.reshape(n_l * n_e, d_in, d_hidden))
    return out.reshape(n_l, n_e, n_f, d_in, TF)


def _dup_heads(x):
    lead = x.shape[:-1]
    x = x.astype(BF16).reshape(*lead, N_KV_HEADS, 1, HEAD_DIM)
    return jnp.broadcast_to(x, (*lead, N_KV_HEADS, 2, HEAD_DIM)).reshape(*lead, 2 * KV_W)


def kernel(x_prompt, x_sample, state_conv, cache_meta_k, cache_meta_v, cache_win_k, cache_win_v, meta_tokens, g_conv, w_pw1, b_pw1, w_dw, b_dw, g_conv_ln, b_conv_ln, w_pw2, b_pw2, g_kv, w_kv, g_k, g_attn, w_q, g_q, sinks, w_o, g_ffn, w_ffn_gate, w_ffn_up, w_ffn_down, g_moe, w_router, w_moe_gate, w_moe_up, w_moe_down):
    bf = lambda w: w.astype(BF16)
    x = jnp.concatenate([x_prompt.reshape(N_PROMPT, D_MODEL), x_sample.reshape(N_SAMPLE, D_MODEL),
                         meta_tokens, jnp.zeros((M_ROWS - ROW_META - N_META, D_MODEL), F32)], axis=0)

    pos = jnp.concatenate([jnp.tile(N_META + jnp.arange(SEQ), BATCH),
                           jnp.tile(N_META + PAST_LEN + jnp.arange(DEC_SEQ), DEC_BATCH),
                           jnp.arange(N_META), jnp.zeros((M_ROWS - ROW_META - N_META,), jnp.int32)])
    half = HEAD_DIM // 2
    inv = jnp.exp(jnp.arange(half, dtype=F32) * (-2.0 * math.log(ROPE_THETA) / HEAD_DIM))
    ang = pos.astype(F32)[:, None] * inv[None, :]
    cos_t = jnp.tile(jnp.concatenate([jnp.cos(ang), jnp.cos(ang)], axis=-1), (1, SLAB // HEAD_DIM))
    sin_t = jnp.tile(jnp.concatenate([-jnp.sin(ang), jnp.sin(ang)], axis=-1), (1, SLAB // HEAD_DIM))
    hd = jnp.arange(SLAB) // HEAD_DIM
    bd = (hd[:, None] == hd[None, :]).astype(BF16)

    dense_expert = jnp.zeros((M_ROWS // TB,), jnp.int32)
    dense_rows = jnp.full((M_ROWS // TB,), TB, jnp.int32)
    conv_p, conv_s = [], []
    w_pw1, w_pw2, w_q, w_o, w_kv = bf(w_pw1), bf(w_pw2), bf(w_q), bf(w_o), bf(w_kv)
    w_ffn_gate, w_ffn_up = _tile_cols(w_ffn_gate[:, None]), _tile_cols(w_ffn_up[:, None])
    w_moe_gate, w_moe_up = _tile_cols(w_moe_gate), _tile_cols(w_moe_up)
    w_ffn_down, w_moe_down = _cast_rows(w_ffn_down[:, None]), _cast_rows(w_moe_down)

    (h,) = _norm(x, gains=(g_conv[0],))
    for l in range(N_A):
        u = _glu(h, w_pw1, b_pw1[l], l)
        u_meta = u[ROW_META:ROW_META + N_META]
        u_samp = u[ROW_SAMPLE:ROW_SAMPLE + N_SAMPLE].reshape(DEC_BATCH, DEC_SEQ, D_MODEL)
        hist_p = jnp.broadcast_to(
            jnp.concatenate([jnp.zeros((HIST - N_META, D_MODEL), F32), u_meta], axis=0)[None],
            (BATCH, HIST, D_MODEL))
        hist_s = jnp.zeros((N_SMALL // DEC_SEQ, HIST, D_MODEL), F32).at[:DEC_BATCH, HIST - (CONV_W - 1):].set(
            state_conv[l])
        conv_args = (w_dw[l], b_dw[l], g_conv_ln[l], b_conv_ln[l])
        c = _conv(u, hist_p, *conv_args, row0=0, n_seq=BATCH, seq_len=SEQ, tt=CONV_TT)
        c = _conv(u, hist_s, *conv_args, row0=ROW_SAMPLE, n_seq=N_SMALL // DEC_SEQ, seq_len=DEC_SEQ,
                  tt=DEC_SEQ, dst=c)
        conv_p.append(jnp.stack([u[(b + 1) * SEQ - (CONV_W - 1):(b + 1) * SEQ] for b in range(BATCH)]))
        conv_s.append(jnp.concatenate([state_conv[l], u_samp], axis=1)[:, -(CONV_W - 1):])
        if l == 0:
            x, h = _resid_mm(c, w_pw2, b_pw2[l], x, l, g_ffn[0])
            y = _ffn(h, dense_expert, dense_rows, w_ffn_gate, w_ffn_up, w_ffn_down, 0)
            x, h = _norm(x, ys=(y,), gains=(g_conv[1],))
        else:
            x, h, idx, gates = _resid_mm(c, w_pw2, b_pw2[l], x, l, g_moe[0], w_router[0])
            y0, y1 = _moe(h, idx, w_moe_gate, w_moe_up, w_moe_down, 0)
            x, h_kv, h = _norm(x, ys=(y0, y1), gates=gates, gains=(g_kv, g_attn[0]))

    k, v = _shared_kv(h_kv, w_kv, g_k, cos_t, sin_t, bd)
    k_fr = k[:N_PROMPT].reshape(BATCH, SEQ, KV_W)
    v_fr = v[:N_PROMPT].reshape(BATCH, SEQ, KV_W)
    k_new = k[ROW_SAMPLE:ROW_SAMPLE + N_SAMPLE].reshape(DEC_BATCH, DEC_SEQ, KV_W)
    v_new = v[ROW_SAMPLE:ROW_SAMPLE + N_SAMPLE].reshape(DEC_BATCH, DEC_SEQ, KV_W)
    k_meta_p = k[ROW_META:ROW_META + N_META]
    v_meta_p = v[ROW_META:ROW_META + N_META]
    front = jnp.zeros((BATCH, WINDOW, KV_W), F32)
    kd_p = _dup_heads(jnp.concatenate([front, k_fr], axis=1))
    vd_p = _dup_heads(jnp.concatenate([front, v_fr], axis=1))
    tail = jnp.zeros((DEC_BATCH, BAND - WINDOW - DEC_SEQ, KV_W), F32)
    pad_b = lambda t: jnp.concatenate(
        [t, jnp.zeros((N_SMALL // DEC_SEQ - DEC_BATCH, *t.shape[1:]), t.dtype)], axis=0)
    kd_s = _dup_heads(pad_b(jnp.concatenate([cache_win_k.reshape(DEC_BATCH, WINDOW, KV_W), k_new, tail], axis=1)))
    vd_s = _dup_heads(pad_b(jnp.concatenate([cache_win_v.reshape(DEC_BATCH, WINDOW, KV_W), v_new, tail], axis=1)))
    kmd_p, vmd_p = _dup_heads(k_meta_p)[None], _dup_heads(v_meta_p)[None]
    kmd_s = _dup_heads(pad_b(cache_meta_k.reshape(DEC_BATCH, N_META, KV_W)))
    vmd_s = _dup_heads(pad_b(cache_meta_v.reshape(DEC_BATCH, N_META, KV_W)))

    for j in range(DEPTH - N_A):
        q = _queries(h, w_q, g_q[j], cos_t, sin_t, bd, j)
        o = _attention(q, kd_p, vd_p, kmd_p, vmd_p, sinks[j], row0=0, n_b=BATCH, n_rows=SEQ, tq=CHUNK,
                       qb=ATT_QB, hi=BAND, prompt=True)
        o = _attention(q, kd_s, vd_s, kmd_s, vmd_s, sinks[j], row0=ROW_SAMPLE, n_b=N_SMALL // DEC_SEQ,
                       n_rows=DEC_SEQ, tq=DEC_SEQ, qb=1, hi=WINDOW + DEC_SEQ, prompt=False, dst=o)
        no_bias = jnp.zeros((D_MODEL,), F32)
        if j == 0:
            x, h = _resid_mm(o, w_o, no_bias, x, j, g_ffn[1])
            y = _ffn(h, dense_expert, dense_rows, w_ffn_gate, w_ffn_up, w_ffn_down, 1)
            x, h = _norm(x, ys=(y,), gains=(g_attn[1],))
        else:
            x, h, idx, gates = _resid_mm(o, w_o, no_bias, x, j, g_moe[1], w_router[1])
            y0, y1 = _moe(h, idx, w_moe_gate, w_moe_up, w_moe_down, 1)
            (y_prompt,) = _norm(x, ys=(y0, y1), gates=gates, tr=256, row0=0, n_rows=N_PROMPT)
            (y_sample,) = _norm(x, ys=(y0, y1), gates=gates, tr=N_SAMPLE, row0=ROW_SAMPLE, n_rows=N_SAMPLE)

    y_prompt = y_prompt.reshape(BATCH, SEQ, D_MODEL)
    y_sample = y_sample.reshape(DEC_BATCH, DEC_SEQ, D_MODEL)
    kv4 = lambda t: t.reshape(*t.shape[:-1], N_KV_HEADS, HEAD_DIM)
    meta_k_p = jnp.broadcast_to(kv4(k_meta_p)[None], (BATCH, N_META, N_KV_HEADS, HEAD_DIM))
    meta_v_p = jnp.broadcast_to(kv4(v_meta_p)[None], (BATCH, N_META, N_KV_HEADS, HEAD_DIM))
    win_k_s = jnp.concatenate([cache_win_k, kv4(k_new)], axis=1)[:, -WINDOW:]
    win_v_s = jnp.concatenate([cache_win_v, kv4(v_new)], axis=1)[:, -WINDOW:]
    return (y_prompt, y_sample, jnp.stack(conv_p), jnp.stack(conv_s), meta_k_p, meta_v_p,
            kv4(k_fr[:, -WINDOW:]), kv4(v_fr[:, -WINDOW:]), win_k_s, win_v_s)
```

```python
import functools
import math

import jax
import jax.numpy as jnp
from jax import lax
from jax.experimental import pallas as pl
from jax.experimental.pallas import tpu as pltpu

D_MODEL = 2048
BATCH = 8
SEQ = 4096
DEPTH = 4
DEC_BATCH = 8
DEC_SEQ = 16
PAST_LEN = 4096
CHUNK = 64
N_META = 16
N_A = DEPTH // 2
CONV_W = 31
HEAD_DIM = 64
N_HEADS = D_MODEL // HEAD_DIM
N_KV_HEADS = N_HEADS // 8
GROUP = N_HEADS // N_KV_HEADS
WINDOW = 128
WIN_CHUNKS = WINDOW // CHUNK
ROPE_THETA = 10000.0
D_FF = 5632
N_EXPERTS = 8
TOP_K = 2
D_EXPERT = 7168
EPS = 1e-6
NEG_INF = -1e30

F32 = jnp.float32
BF16 = jnp.bfloat16

N_PROMPT = BATCH * SEQ
N_SAMPLE = DEC_BATCH * DEC_SEQ
N_SMALL = 256
M_ROWS = N_PROMPT + N_SMALL
ROW_SAMPLE = N_PROMPT
ROW_META = N_PROMPT + N_SAMPLE

TM = 768
TN = 1024
TF = 512
TB = TM
N_ASSIGN = M_ROWS * TOP_K
N_BLOCKS = N_ASSIGN // TB + N_EXPERTS
TR = 384
SLAB = 256
HIST = 32
CONV_TT = 128
CONV_RG = 128
CONV_LS = 128
FFN_SUBS = (384, 384)
TMR = 384
R_SUB = 192
Q_SUB = 256
CAST_ROWS = 256
KV_W = N_KV_HEADS * HEAD_DIM
KEYS = 256
BAND = (WIN_CHUNKS + 1) * CHUNK
ATT_QB = 8
VMEM_LIMIT = 56 * 1024 * 1024


def _cparams(sem):
    return pltpu.CompilerParams(dimension_semantics=sem, vmem_limit_bytes=VMEM_LIMIT)


def _dot(a, b):
    return jnp.dot(a, b, preferred_element_type=F32)


def _norm_kernel(*refs, n_y, gated, n_g, write_x):
    x_ref = refs[0]
    y_refs = refs[1:1 + n_y]
    pos = 1 + n_y
    gate_ref = refs[pos] if gated else None
    pos += 1 if gated else 0
    g_refs = refs[pos:pos + n_g]
    pos += n_g
    outs = refs[pos:]
    x = x_ref[...]
    for i, y_ref in enumerate(y_refs):
        y = y_ref[...]
        x = x + (gate_ref[:, i:i + 1] * y if gated else y)
    o = 0
    if write_x:
        outs[0][...] = x
        o = 1
    if n_g:
        xn = x * lax.rsqrt(jnp.mean(x * x, axis=-1, keepdims=True) + EPS)
        for i, g_ref in enumerate(g_refs):
            outs[o + i][...] = (xn * g_ref[...]).astype(BF16)


def _norm(x, ys=(), gates=None, gains=(), *, tr=TR, row0=0, n_rows=M_ROWS):
    n_y, n_g = len(ys), len(gains)
    write_x = n_y > 0
    blk0 = row0 // tr
    row_in = pl.BlockSpec((tr, D_MODEL), lambda i: (blk0 + i, 0))
    row_out = pl.BlockSpec((tr, D_MODEL), lambda i: (i, 0))
    in_specs = [row_in] * (1 + n_y)
    args = [x, *ys]
    if gates is not None:
        in_specs.append(pl.BlockSpec((tr, 128), lambda i: (blk0 + i, 0)))
        args.append(gates)
    for g in gains:
        in_specs.append(pl.BlockSpec((1, D_MODEL), lambda i: (0, 0)))
        args.append(g.reshape(1, D_MODEL))
    out_shape, out_specs = [], []
    if write_x:
        out_shape.append(jax.ShapeDtypeStruct((n_rows, D_MODEL), F32))
        out_specs.append(row_out)
    for _ in gains:
        out_shape.append(jax.ShapeDtypeStruct((n_rows, D_MODEL), BF16))
        out_specs.append(row_out)
    return pl.pallas_call(
        functools.partial(_norm_kernel, n_y=n_y, gated=gates is not None, n_g=n_g, write_x=write_x),
        grid=(n_rows // tr,), in_specs=in_specs, out_specs=out_specs, out_shape=out_shape,
        compiler_params=_cparams(("parallel",)), name="norm")(*args)


def _glu_kernel(h_ref, wa_ref, wb_ref, ba_ref, bb_ref, u_ref):
    h = h_ref[...]
    a = _dot(h, wa_ref[0]) + ba_ref[...]
    b = _dot(h, wb_ref[0]) + bb_ref[...]
    u_ref[...] = a * jax.nn.sigmoid(b)


def _glu(h, w, b, layer):
    nj = D_MODEL // TN
    b2 = b.reshape(1, 2 * D_MODEL)
    return pl.pallas_call(
        _glu_kernel, grid=(nj, M_ROWS // TM),
        in_specs=[pl.BlockSpec((TM, D_MODEL), lambda j, i: (i, 0)),
                  pl.BlockSpec((1, D_MODEL, TN), lambda j, i: (layer, 0, j)),
                  pl.BlockSpec((1, D_MODEL, TN), lambda j, i: (layer, 0, j + nj)),
                  pl.BlockSpec((1, TN), lambda j, i: (0, j)),
                  pl.BlockSpec((1, TN), lambda j, i: (0, j + nj))],
        out_specs=pl.BlockSpec((TM, TN), lambda j, i: (i, j)),
        out_shape=jax.ShapeDtypeStruct((M_ROWS, D_MODEL), F32),
        compiler_params=_cparams(("parallel", "parallel")), name="pw1_glu")(h, w, w, b2, b2)


def _top2(h, wr_ref, idx_ref, gate_ref, rows):
    h_hi = h.astype(BF16)
    h_lo = (h - h_hi.astype(F32)).astype(BF16)
    logits = _dot(h_hi, wr_ref[0]) + (_dot(h_hi, wr_ref[1]) + _dot(h_lo, wr_ref[0]))
    lane_i = lax.broadcasted_iota(jnp.int32, logits.shape, 1)
    lane = lane_i.astype(F32)
    logits = jnp.where(lane_i < N_EXPERTS, logits, -jnp.inf)
    m1 = jnp.max(logits, axis=-1, keepdims=True)
    i1 = jnp.min(jnp.where(logits == m1, lane, 128.0), axis=-1, keepdims=True)
    rest = jnp.where(lane == i1, -jnp.inf, logits)
    m2 = jnp.max(rest, axis=-1, keepdims=True)
    i2 = jnp.min(jnp.where(rest == m2, lane, 128.0), axis=-1, keepdims=True)
    e2 = jnp.exp(m2 - m1)
    g1 = 1.0 / (1.0 + e2)
    idx_ref[rows, :] = jnp.where(lane_i == 0, i1, i2).astype(jnp.int32)
    gate_ref[rows, :] = jnp.where(lane_i == 0, g1, e2 * g1)


def _resid_kernel(*refs, routed, n_after):
    n_in = 6 if routed else 5
    refs = refs[:n_in] + refs[n_in + n_after:]
    if routed:
        x_ref, w_ref, b_ref, r_ref, g_ref, wr_ref, xo_ref, h_ref, idx_ref, gate_ref = refs
    else:
        x_ref, w_ref, b_ref, r_ref, g_ref, xo_ref, h_ref = refs
    for s in range(TMR // R_SUB):
        rows = slice(s * R_SUB, (s + 1) * R_SUB)
        xn = r_ref[rows, :] + (_dot(x_ref[rows, :], w_ref[0]) + b_ref[...])
        xo_ref[rows, :] = xn
        h = xn * lax.rsqrt(jnp.mean(xn * xn, axis=-1, keepdims=True) + EPS) * g_ref[...]
        h_ref[rows, :] = h.astype(BF16)
        if routed:
            _top2(h, wr_ref, idx_ref, gate_ref, rows)


def _resid_mm(x, w, b, r, layer, gain, w_router=None, after=()):
    routed = w_router is not None
    row = lambda width: pl.BlockSpec((TMR, width), lambda i: (i, 0))
    vec = pl.BlockSpec((1, D_MODEL), lambda i: (0, 0))
    in_specs = [row(D_MODEL), pl.BlockSpec((1, D_MODEL, D_MODEL), lambda i: (layer, 0, 0)), vec, row(D_MODEL), vec]
    args = [x, w, b.reshape(1, D_MODEL), r, gain.reshape(1, D_MODEL)]
    out_specs = [row(D_MODEL), row(D_MODEL)]
    out_shape = [jax.ShapeDtypeStruct((M_ROWS, D_MODEL), F32), jax.ShapeDtypeStruct((M_ROWS, D_MODEL), BF16)]
    if routed:
        in_specs.append(pl.BlockSpec((2, D_MODEL, 128), lambda i: (0, 0, 0)))
        wr = jnp.zeros((D_MODEL, 128), F32).at[:, :N_EXPERTS].set(w_router)
        wr_hi = wr.astype(BF16)
        args.append(jnp.stack([wr_hi, (wr - wr_hi.astype(F32)).astype(BF16)]))
        out_specs += [row(128), row(128)]
        out_shape += [jax.ShapeDtypeStruct((M_ROWS, 128), jnp.int32), jax.ShapeDtypeStruct((M_ROWS, 128), F32)]
    in_specs += [pl.BlockSpec(memory_space=pl.ANY)] * len(after)
    args += list(after)
    return pl.pallas_call(
        functools.partial(_resid_kernel, routed=routed, n_after=len(after)), grid=(M_ROWS // TMR,),
        in_specs=in_specs, out_specs=out_specs, out_shape=out_shape,
        compiler_params=_cparams(("parallel",)), name="proj_resid_norm")(*args)


def _head_norm_rope(x, gain, cos, sin, bd):
    ssq = _dot((x * x).astype(BF16), bd)
    xn = x * lax.rsqrt(ssq * (1.0 / HEAD_DIM) + EPS) * gain
    lane = lax.broadcasted_iota(jnp.int32, xn.shape, 1)
    first_half = (lane % HEAD_DIM) < (HEAD_DIM // 2)
    rot = jnp.where(first_half, pltpu.roll(xn, SLAB - HEAD_DIM // 2, 1), pltpu.roll(xn, HEAD_DIM // 2, 1))
    return xn * cos + rot * sin


def _q_kernel(h_ref, w_ref, g_ref, cos_ref, sin_ref, bd_ref, q_ref):
    bd, gain = bd_ref[...], g_ref[...]
    scale = HEAD_DIM ** -0.5
    for r in range(TM // Q_SUB):
        rows = slice(r * Q_SUB, (r + 1) * Q_SUB)
        acc = _dot(h_ref[rows, :], w_ref[0])
        cos, sin = cos_ref[rows, :], sin_ref[rows, :]
        for s in range(TN // SLAB):
            q = _head_norm_rope(acc[:, s * SLAB:(s + 1) * SLAB], gain, cos, sin, bd)
            q_ref[rows, s * SLAB:(s + 1) * SLAB] = (q * scale).astype(BF16)


def _queries(h, w, g_q, cos, sin, bd, layer):
    return pl.pallas_call(
        _q_kernel, grid=(D_MODEL // TN, M_ROWS // TM),
        in_specs=[pl.BlockSpec((TM, D_MODEL), lambda j, i: (i, 0)),
                  pl.BlockSpec((1, D_MODEL, TN), lambda j, i: (layer, 0, j)),
                  pl.BlockSpec((1, SLAB), lambda j, i: (0, 0)),
                  pl.BlockSpec((TM, SLAB), lambda j, i: (i, 0)),
                  pl.BlockSpec((TM, SLAB), lambda j, i: (i, 0)),
                  pl.BlockSpec((SLAB, SLAB), lambda j, i: (0, 0))],
        out_specs=pl.BlockSpec((TM, TN), lambda j, i: (i, j)),
        out_shape=jax.ShapeDtypeStruct((M_ROWS, D_MODEL), BF16),
        compiler_params=_cparams(("parallel", "parallel")), name="q_proj")(
            h, w, jnp.tile(g_q, SLAB // HEAD_DIM).reshape(1, SLAB), cos, sin, bd)


def _kv_kernel(h_ref, w_ref, g_ref, cos_ref, sin_ref, bd_ref, k_ref, v_ref):
    acc = _dot(h_ref[...], w_ref[...])
    k_ref[...] = _head_norm_rope(acc[:, :KV_W], g_ref[...], cos_ref[...], sin_ref[...], bd_ref[...])
    v_ref[...] = acc[:, KV_W:]


def _shared_kv(h, w, g_k, cos, sin, bd):
    return pl.pallas_call(
        _kv_kernel, grid=(M_ROWS // TM,),
        in_specs=[pl.BlockSpec((TM, D_MODEL), lambda i: (i, 0)),
                  pl.BlockSpec((D_MODEL, 2 * KV_W), lambda i: (0, 0)),
                  pl.BlockSpec((1, SLAB), lambda i: (0, 0)),
                  pl.BlockSpec((TM, SLAB), lambda i: (i, 0)),
                  pl.BlockSpec((TM, SLAB), lambda i: (i, 0)),
                  pl.BlockSpec((SLAB, SLAB), lambda i: (0, 0))],
        out_specs=[pl.BlockSpec((TM, KV_W), lambda i: (i, 0))] * 2,
        out_shape=[jax.ShapeDtypeStruct((M_ROWS, KV_W), F32)] * 2,
        compiler_params=_cparams(("parallel",)), name="kv_proj")(
            h, w, jnp.tile(g_k, SLAB // HEAD_DIM).reshape(1, SLAB), cos, sin, bd)


def _conv_kernel(*refs, tt, use_prev, has_dst):
    c_ref, ext_ref, acc_ref = refs[-3:]
    ins = refs[:-4] if has_dst else refs[:-3]
    if use_prev:
        u_ref, prev_ref, hist_ref, w_ref, bdw_ref, gln_ref, bln_ref = ins
    else:
        u_ref, hist_ref, w_ref, bdw_ref, gln_ref, bln_ref = ins
    if use_prev:
        first = pl.program_id(1) == 0

        @pl.when(first)
        def _():
            ext_ref[0:HIST, :] = hist_ref[0]

        @pl.when(jnp.logical_not(first))
        def _():
            ext_ref[0:HIST, :] = prev_ref[...]
    else:
        ext_ref[0:HIST, :] = hist_ref[0]
    ext_ref[HIST:HIST + tt, :] = u_ref[...]
    rg = min(CONV_RG, tt)
    off = HIST - (CONV_W - 1)
    for r in range(tt // rg):
        for s in range(D_MODEL // CONV_LS):
            ls = slice(s * CONV_LS, (s + 1) * CONV_LS)
            acc = jnp.zeros((rg, CONV_LS), F32)
            for res in range(8):
                offs = [o for o in range(off, off + CONV_W) if o % 8 == res]
                win = ext_ref[r * rg + offs[0]:r * rg + offs[-1] + rg, ls]
                part = None
                for o in offs:
                    k = o - off
                    term = win[o - offs[0]:o - offs[0] + rg, :] * w_ref[k:k + 1, ls]
                    part = term if part is None else part + term
                acc = acc + part
            acc_ref[r * rg:(r + 1) * rg, ls] = acc
    x = acc_ref[...] + bdw_ref[...]
    xc = x - jnp.mean(x, axis=-1, keepdims=True)
    y = xc * lax.rsqrt(jnp.mean(xc * xc, axis=-1, keepdims=True) + EPS)
    y = y * gln_ref[...] + bln_ref[...]
    c_ref[...] = (y * jax.nn.sigmoid(y)).astype(BF16)


def _conv(u, hist, w_dw, b_dw, g_ln, b_ln, *, row0, n_seq, seq_len, tt, dst=None):
    n_t = seq_len // tt
    use_prev = n_t > 1
    blk0 = row0 // tt
    wpad = jnp.zeros((32, D_MODEL), F32).at[:CONV_W].set(w_dw)
    vec = pl.BlockSpec((1, D_MODEL), lambda s, i: (0, 0))
    in_specs = [pl.BlockSpec((tt, D_MODEL), lambda s, i: (blk0 + s * n_t + i, 0))]
    args = [u]
    if use_prev:
        per = tt // HIST
        in_specs.append(pl.BlockSpec(
            (HIST, D_MODEL), lambda s, i: (jnp.maximum((blk0 + s * n_t + i) * per - 1, 0), 0)))
        args.append(u)
    in_specs += [pl.BlockSpec((1, HIST, D_MODEL), lambda s, i: (s, 0, 0)),
                 pl.BlockSpec((32, D_MODEL), lambda s, i: (0, 0)), vec, vec, vec]
    args += [hist, wpad, b_dw.reshape(1, D_MODEL), g_ln.reshape(1, D_MODEL), b_ln.reshape(1, D_MODEL)]
    aliases = {}
    if dst is not None:
        aliases = {len(args): 0}
        in_specs.append(pl.BlockSpec(memory_space=pl.ANY))
        args.append(dst)
    return pl.pallas_call(
        functools.partial(_conv_kernel, tt=tt, use_prev=use_prev, has_dst=dst is not None), grid=(n_seq, n_t),
        in_specs=in_specs,
        out_specs=pl.BlockSpec((tt, D_MODEL), lambda s, i: (blk0 + s * n_t + i, 0)),
        out_shape=jax.ShapeDtypeStruct((M_ROWS, D_MODEL), BF16),
        input_output_aliases=aliases,
        scratch_shapes=[pltpu.VMEM((HIST + tt, D_MODEL), F32), pltpu.VMEM((tt, D_MODEL), F32)],
        compiler_params=_cparams(("parallel", "arbitrary")), name="conv_ln_swish")(*args)


def _ffn_kernel(be_ref, nv_ref, x_ref, wg_ref, wu_ref, wd_ref, *rest):
    o_ref, acc_ref = rest[-2:]
    i, f = pl.program_id(0), pl.program_id(1)
    n_valid = nv_ref[i]

    @pl.when(f == 0)
    def _():
        acc_ref[...] = jnp.zeros_like(acc_ref)

    def run(subs):
        r0 = 0
        for n in subs:
            rows = slice(r0, r0 + n)
            r0 += n
            x = x_ref[rows, :]
            g = _dot(x, wg_ref[0, 0, 0])
            u = _dot(x, wu_ref[0, 0, 0])
            a = (g * jax.nn.sigmoid(g) * u).astype(BF16)
            acc_ref[rows, :] += _dot(a, wd_ref[0, 0])

    @pl.when(n_valid > FFN_SUBS[0])
    def _():
        run(FFN_SUBS)

    @pl.when(jnp.logical_and(n_valid > 0, n_valid <= FFN_SUBS[0]))
    def _():
        run(FFN_SUBS[:1])

    @pl.when(f == pl.num_programs(1) - 1)
    def _():
        o_ref[...] = acc_ref[...].astype(BF16)


def _ffn(x, block_expert, block_rows, wg, wu, wd, layer, *, out_blocks=None, blk0=0, dst=None):
    n_blocks = x.shape[0] // TB
    out_blocks = n_blocks if out_blocks is None else out_blocks
    n_f = wg.shape[2]

    def f_idx(i, f, ok):
        return jnp.where(ok[i] > 0, f, n_f - 1)

    in_specs = [pl.BlockSpec((TB, D_MODEL), lambda i, f, be, ok: (i, 0)),
                pl.BlockSpec((1, 1, 1, D_MODEL, TF),
                             lambda i, f, be, ok: (layer, be[i], f_idx(i, f, ok), 0, 0)),
                pl.BlockSpec((1, 1, 1, D_MODEL, TF),
                             lambda i, f, be, ok: (layer, be[i], f_idx(i, f, ok), 0, 0)),
                pl.BlockSpec((1, 1, TF, D_MODEL), lambda i, f, be, ok: (layer, be[i], f_idx(i, f, ok), 0))]
    args = [block_expert, block_rows, x, wg, wu, wd]
    aliases = {}
    if dst is not None:
        aliases = {len(args): 0}
        in_specs.append(pl.BlockSpec(memory_space=pl.ANY))
        args.append(dst)
    grid_spec = pltpu.PrefetchScalarGridSpec(
        num_scalar_prefetch=2, grid=(n_blocks, n_f), in_specs=in_specs,
        out_specs=pl.BlockSpec((TB, D_MODEL), lambda i, f, be, ok: (blk0 + i, 0)),
        scratch_shapes=[pltpu.VMEM((TB, D_MODEL), F32)])
    return pl.pallas_call(
        _ffn_kernel, grid_spec=grid_spec,
        out_shape=jax.ShapeDtypeStruct((out_blocks * TB, D_MODEL), BF16), input_output_aliases=aliases,
        compiler_params=_cparams(("parallel", "arbitrary")), name="swiglu")(*args)


def _moe(h, idx, wg, wu, wd, layer):
    e_flat = idx[:, :TOP_K].reshape(N_ASSIGN)
    onehot = (e_flat[:, None] == jnp.arange(N_EXPERTS, dtype=jnp.int32)[None, :]).astype(jnp.int32)
    csum = jnp.cumsum(onehot, axis=0)
    rank = jnp.sum(csum * onehot, axis=-1) - 1
    counts = csum[-1]
    padded = (counts + TB - 1) // TB * TB
    pad_ends = jnp.cumsum(padded)
    pad_starts = pad_ends - padded
    dest = pad_starts[e_flat] + rank
    blk_start = jnp.arange(N_BLOCKS, dtype=jnp.int32) * TB
    block_expert = jnp.minimum(jnp.searchsorted(pad_ends, blk_start, side='right'),
                               N_EXPERTS - 1).astype(jnp.int32)
    real_end = pad_starts + counts
    block_rows = jnp.where(blk_start < pad_ends[-1],
                           jnp.clip(real_end[block_expert] - blk_start, 0, TB), 0).astype(jnp.int32)
    src = jnp.zeros((N_BLOCKS * TB,), jnp.int32).at[dest].set(jnp.arange(N_ASSIGN, dtype=jnp.int32) // TOP_K)
    half = N_BLOCKS // 2
    y_buf = _ffn(h[src[:half * TB]], block_expert[:half], block_rows[:half], wg, wu, wd, layer,
                 out_blocks=N_BLOCKS)
    y_buf = _ffn(h[src[half * TB:]], block_expert[half:], block_rows[half:], wg, wu, wd, layer,
                 out_blocks=N_BLOCKS, blk0=half, dst=y_buf)
    d2 = dest.reshape(M_ROWS, TOP_K)
    return y_buf[d2[:, 0]], y_buf[d2[:, 1]]


def _attn_kernel(*refs, tq, qb, hi, prompt):
    q_ref, k_ref, v_ref, km_ref, vm_ref, sink_ref = refs[:6]
    o_ref = refs[-1]
    _attn_body(q_ref, k_ref, v_ref, km_ref, vm_ref, sink_ref, o_ref, tq=tq, qb=qb, hi=hi, prompt=prompt)


def _attn_body(q_ref, k_ref, v_ref, km_ref, vm_ref, sink_ref, o_ref, *, tq, qb, hi, prompt):
    lane = lax.broadcasted_iota(jnp.int32, (KEYS, 128), 1)
    low = lane < HEAD_DIM
    key = lax.broadcasted_iota(jnp.int32, (1, 2 * KEYS), 1) % KEYS
    is_meta = jnp.logical_and(key >= BAND, key < BAND + N_META)
    zpad_k = jnp.zeros((KEYS - BAND - N_META, 128), BF16)
    ones_row = lax.broadcasted_iota(jnp.int32, (2 * KEYS, 128), 0) < KEYS
    ones_low = lax.broadcasted_iota(jnp.int32, (2 * KEYS, 128), 1) < HEAD_DIM
    sum_cols = jnp.where(ones_row == ones_low, 1.0, 0.0).astype(BF16)
    out_low = lax.broadcasted_iota(jnp.int32, (4 * tq, 128), 1) < HEAD_DIM

    def chunk(cc, carry):
        c_glob = pl.program_id(1) * qb + cc
        row0 = pl.multiple_of(cc * tq, tq)
        key0 = pl.multiple_of(c_glob * CHUNK, CHUNK) if prompt else 0
        lo = jnp.maximum(WIN_CHUNKS - c_glob, 0) * CHUNK if prompt else 0
        ok = jnp.logical_or(jnp.logical_and(key >= lo, key < hi), is_meta)
        bias = jnp.where(ok, 0.0, NEG_INF)
        for g in range(N_KV_HEADS):
            gl = slice(g * 128, (g + 1) * 128)
            k_all = jnp.concatenate([k_ref[0, pl.ds(key0, BAND), gl], km_ref[0, :, gl], zpad_k], axis=0)
            v_all = jnp.concatenate([v_ref[0, pl.ds(key0, BAND), gl], vm_ref[0, :, gl], zpad_k], axis=0)
            k2 = jnp.concatenate([jnp.where(low, k_all, 0), jnp.where(low, 0, k_all)], axis=0)
            v2 = jnp.concatenate([jnp.where(low, v_all, 0), jnp.where(low, 0, v_all)], axis=0)
            qp = jnp.concatenate(
                [q_ref[pl.ds(row0, tq), g * 512 + i * 128:g * 512 + (i + 1) * 128] for i in range(4)], axis=0)
            s = lax.dot_general(qp, k2, (((1,), (1,)), ((), ())), preferred_element_type=F32) + bias
            es, sink_e = [], []
            for p in range(2):
                sp = s[:, p * KEYS:(p + 1) * KEYS]
                sk = sink_ref[g, :, p:p + 1]
                m = jnp.maximum(jnp.max(sp, axis=-1, keepdims=True), sk)
                es.append(jnp.exp(sp - m).astype(BF16))
                sink_e.append(jnp.exp(sk - m))
            ox = _dot(jnp.concatenate(es, axis=1), jnp.concatenate([v2, sum_cols], axis=1))
            den = ox[:, 128:] + jnp.where(out_low, sink_e[0], sink_e[1])
            o = ox[:, :128] * (1.0 / den)
            for i in range(4):
                o_ref[pl.ds(row0, tq), g * 512 + i * 128:g * 512 + (i + 1) * 128] = (
                    o[i * tq:(i + 1) * tq, :].astype(BF16))
        return carry

    lax.fori_loop(0, qb, chunk, 0, unroll=min(4, qb))


def _attention(q, kd, vd, kmd, vmd, sinks, *, row0, n_b, n_rows, tq, qb, hi, prompt, dst=None):
    blk_rows = tq * qb
    n_i = n_rows // blk_rows
    blk0 = row0 // blk_rows
    per_b_meta = kmd.shape[0] > 1
    sk = sinks.reshape(N_KV_HEADS, 4, 1, 2)
    sk = jnp.broadcast_to(sk, (N_KV_HEADS, 4, tq, 2)).reshape(N_KV_HEADS, 4 * tq, 2)
    n_keys = kd.shape[1]
    kv_spec = pl.BlockSpec((1, n_keys, 512), lambda b, i: (b, 0, 0))
    meta_spec = pl.BlockSpec((1, N_META, 512), lambda b, i: (b if per_b_meta else 0, 0, 0))
    row_spec = pl.BlockSpec((blk_rows, D_MODEL), lambda b, i: (blk0 + b * n_i + i, 0))
    in_specs = [row_spec, kv_spec, kv_spec, meta_spec, meta_spec,
                pl.BlockSpec((N_KV_HEADS, 4 * tq, 2), lambda b, i: (0, 0, 0))]
    args = [q, kd, vd, kmd, vmd, sk]
    aliases = {}
    if dst is not None:
        aliases = {len(args): 0}
        in_specs.append(pl.BlockSpec(memory_space=pl.ANY))
        args.append(dst)
    return pl.pallas_call(
        functools.partial(_attn_kernel, tq=tq, qb=qb, hi=hi, prompt=prompt),
        grid=(n_b, n_i), in_specs=in_specs, out_specs=row_spec,
        out_shape=jax.ShapeDtypeStruct((M_ROWS, D_MODEL), BF16), input_output_aliases=aliases,
        compiler_params=_cparams(("parallel", "arbitrary")), name="swa")(*args)


def _cast_kernel(w_ref, o_ref):
    o_ref[...] = w_ref[...].astype(BF16)


def _cast_rows(w):
    n_l, n_e, d_hidden, d_out = w.shape
    spec = pl.BlockSpec((1, TF, d_out), lambda e, i: (e, i, 0))
    out = pl.pallas_call(
        _cast_kernel, grid=(n_l * n_e, d_hidden // TF), in_specs=[spec], out_specs=spec,
        out_shape=jax.ShapeDtypeStruct((n_l * n_e, d_hidden, d_out), BF16),
        compiler_params=_cparams(("parallel", "parallel")), name="cast")(w.reshape(n_l * n_e, d_hidden, d_out))
    return out.reshape(n_l, n_e, d_hidden, d_out)


def _tile_cols_kernel(w_ref, o_ref):
    for f in range(o_ref.shape[1]):
        o_ref[0, f] = w_ref[0, :, f * TF:(f + 1) * TF].astype(BF16)


def _tile_cols(w):
    n_l, n_e, d_in, d_hidden = w.shape
    n_f = d_hidden // TF
    out = pl.pallas_call(
        _tile_cols_kernel, grid=(n_l * n_e, d_in // CAST_ROWS),
        in_specs=[pl.BlockSpec((1, CAST_ROWS, d_hidden), lambda e, i: (e, i, 0))],
        out_specs=pl.BlockSpec((1, n_f, CAST_ROWS, TF), lambda e, i: (e, 0, i, 0)),
        out_shape=jax.ShapeDtypeStruct((n_l * n_e, n_f, d_in, TF), BF16),
        compiler_params=_cparams(("parallel", "parallel")), name="cast_tile")(w.reshape(n_l * n_e, d_in, d_hidden))
    return out.reshape(n_l, n_e, n_f, d_in, TF)


def _dup_heads(x):
    lead = x.shape[:-1]
    x = x.astype(BF16).reshape(*lead, N_KV_HEADS, 1, HEAD_DIM)
    return jnp.broadcast_to(x, (*lead, N_KV_HEADS, 2, HEAD_DIM)).reshape(*lead, 2 * KV_W)


def kernel(x_prompt, x_sample, state_conv, cache_meta_k, cache_meta_v, cache_win_k, cache_win_v, meta_tokens, g_conv, w_pw1, b_pw1, w_dw, b_dw, g_conv_ln, b_conv_ln, w_pw2, b_pw2, g_kv, w_kv, g_k, g_attn, w_q, g_q, sinks, w_o, g_ffn, w_ffn_gate, w_ffn_up, w_ffn_down, g_moe, w_router, w_moe_gate, w_moe_up, w_moe_down):
    bf = lambda w: w.astype(BF16)
    x = jnp.concatenate([x_prompt.reshape(N_PROMPT, D_MODEL), x_sample.reshape(N_SAMPLE, D_MODEL),
                         meta_tokens, jnp.zeros((M_ROWS - ROW_META - N_META, D_MODEL), F32)], axis=0)

    pos = jnp.concatenate([jnp.tile(N_META + jnp.arange(SEQ), BATCH),
                           jnp.tile(N_META + PAST_LEN + jnp.arange(DEC_SEQ), DEC_BATCH),
                           jnp.arange(N_META), jnp.zeros((M_ROWS - ROW_META - N_META,), jnp.int32)])
    half = HEAD_DIM // 2
    inv = jnp.exp(jnp.arange(half, dtype=F32) * (-2.0 * math.log(ROPE_THETA) / HEAD_DIM))
    ang = pos.astype(F32)[:, None] * inv[None, :]
    cos_t = jnp.tile(jnp.concatenate([jnp.cos(ang), jnp.cos(ang)], axis=-1), (1, SLAB // HEAD_DIM))
    sin_t = jnp.tile(jnp.concatenate([-jnp.sin(ang), jnp.sin(ang)], axis=-1), (1, SLAB // HEAD_DIM))
    hd = jnp.arange(SLAB) // HEAD_DIM
    bd = (hd[:, None] == hd[None, :]).astype(BF16)

    dense_expert = jnp.zeros((M_ROWS // TB,), jnp.int32)
    dense_rows = jnp.full((M_ROWS // TB,), TB, jnp.int32)
    conv_p, conv_s = [], []
    w_pw1, w_pw2, w_q, w_o, w_kv = bf(w_pw1), bf(w_pw2), bf(w_q), bf(w_o), bf(w_kv)
    w_ffn_gate, w_ffn_up = _tile_cols(w_ffn_gate[:, None]), _tile_cols(w_ffn_up[:, None])
    w_moe_gate, w_moe_up = _tile_cols(w_moe_gate), _tile_cols(w_moe_up)
    w_ffn_down, w_moe_down = _cast_rows(w_ffn_down[:, None]), _cast_rows(w_moe_down)

    (h,) = _norm(x, gains=(g_conv[0],))
    for l in range(N_A):
        u = _glu(h, w_pw1, b_pw1[l], l)
        u_meta = u[ROW_META:ROW_META + N_META]
        u_samp = u[ROW_SAMPLE:ROW_SAMPLE + N_SAMPLE].reshape(DEC_BATCH, DEC_SEQ, D_MODEL)
        hist_p = jnp.broadcast_to(
            jnp.concatenate([jnp.zeros((HIST - N_META, D_MODEL), F32), u_meta], axis=0)[None],
            (BATCH, HIST, D_MODEL))
        hist_s = jnp.zeros((N_SMALL // DEC_SEQ, HIST, D_MODEL), F32).at[:DEC_BATCH, HIST - (CONV_W - 1):].set(
            state_conv[l])
        conv_args = (w_dw[l], b_dw[l], g_conv_ln[l], b_conv_ln[l])
        c = _conv(u, hist_p, *conv_args, row0=0, n_seq=BATCH, seq_len=SEQ, tt=CONV_TT)
        c = _conv(u, hist_s, *conv_args, row0=ROW_SAMPLE, n_seq=N_SMALL // DEC_SEQ, seq_len=DEC_SEQ,
                  tt=DEC_SEQ, dst=c)
        conv_p.append(jnp.stack([u[(b + 1) * SEQ - (CONV_W - 1):(b + 1) * SEQ] for b in range(BATCH)]))
        conv_s.append(jnp.concatenate([state_conv[l], u_samp], axis=1)[:, -(CONV_W - 1):])
        if l == 0:
            x, h = _resid_mm(c, w_pw2, b_pw2[l], x, l, g_ffn[0])
            y = _ffn(h, dense_expert, dense_rows, w_ffn_gate, w_ffn_up, w_ffn_down, 0)
            x, h = _norm(x, ys=(y,), gains=(g_conv[1],))
        else:
            x, h, idx, gates = _resid_mm(c, w_pw2, b_pw2[l], x, l, g_moe[0], w_router[0],
                                         after=(w_moe_gate, w_moe_up, w_moe_down))
            y0, y1 = _moe(h, idx, w_moe_gate, w_moe_up, w_moe_down, 0)
            x, h_kv, h = _norm(x, ys=(y0, y1), gates=gates, gains=(g_kv, g_attn[0]))

    k, v = _shared_kv(h_kv, w_kv, g_k, cos_t, sin_t, bd)
    k_fr = k[:N_PROMPT].reshape(BATCH, SEQ, KV_W)
    v_fr = v[:N_PROMPT].reshape(BATCH, SEQ, KV_W)
    k_new = k[ROW_SAMPLE:ROW_SAMPLE + N_SAMPLE].reshape(DEC_BATCH, DEC_SEQ, KV_W)
    v_new = v[ROW_SAMPLE:ROW_SAMPLE + N_SAMPLE].reshape(DEC_BATCH, DEC_SEQ, KV_W)
    k_meta_p = k[ROW_META:ROW_META + N_META]
    v_meta_p = v[ROW_META:ROW_META + N_META]
    front = jnp.zeros((BATCH, WINDOW, KV_W), F32)
    kd_p = _dup_heads(jnp.concatenate([front, k_fr], axis=1))
    vd_p = _dup_heads(jnp.concatenate([front, v_fr], axis=1))
    tail = jnp.zeros((DEC_BATCH, BAND - WINDOW - DEC_SEQ, KV_W), F32)
    pad_b = lambda t: jnp.concatenate(
        [t, jnp.zeros((N_SMALL // DEC_SEQ - DEC_BATCH, *t.shape[1:]), t.dtype)], axis=0)
    kd_s = _dup_heads(pad_b(jnp.concatenate([cache_win_k.reshape(DEC_BATCH, WINDOW, KV_W), k_new, tail], axis=1)))
    vd_s = _dup_heads(pad_b(jnp.concatenate([cache_win_v.reshape(DEC_BATCH, WINDOW, KV_W), v_new, tail], axis=1)))
    kmd_p, vmd_p = _dup_heads(k_meta_p)[None], _dup_heads(v_meta_p)[None]
    kmd_s = _dup_heads(pad_b(cache_meta_k.reshape(DEC_BATCH, N_META, KV_W)))
    vmd_s = _dup_heads(pad_b(cache_meta_v.reshape(DEC_BATCH, N_META, KV_W)))

    for j in range(DEPTH - N_A):
        q = _queries(h, w_q, g_q[j], cos_t, sin_t, bd, j)
        o = _attention(q, kd_p, vd_p, kmd_p, vmd_p, sinks[j], row0=0, n_b=BATCH, n_rows=SEQ, tq=CHUNK,
                       qb=ATT_QB, hi=BAND, prompt=True)
        o = _attention(q, kd_s, vd_s, kmd_s, vmd_s, sinks[j], row0=ROW_SAMPLE, n_b=N_SMALL // DEC_SEQ,
                       n_rows=DEC_SEQ, tq=DEC_SEQ, qb=1, hi=WINDOW + DEC_SEQ, prompt=False, dst=o)
        no_bias = jnp.zeros((D_MODEL,), F32)
        if j == 0:
            x, h = _resid_mm(o, w_o, no_bias, x, j, g_ffn[1])
            y = _ffn(h, dense_expert, dense_rows, w_ffn_gate, w_ffn_up, w_ffn_down, 1)
            x, h = _norm(x, ys=(y,), gains=(g_attn[1],))
        else:
            x, h, idx, gates = _resid_mm(o, w_o, no_bias, x, j, g_moe[1], w_router[1])
            y0, y1 = _moe(h, idx, w_moe_gate, w_moe_up, w_moe_down, 1)
            (y_prompt,) = _norm(x, ys=(y0, y1), gates=gates, tr=256, row0=0, n_rows=N_PROMPT)
            (y_sample,) = _norm(x, ys=(y0, y1), gates=gates, tr=N_SAMPLE, row0=ROW_SAMPLE, n_rows=N_SAMPLE)

    y_prompt = y_prompt.reshape(BATCH, SEQ, D_MODEL)
    y_sample = y_sample.reshape(DEC_BATCH, DEC_SEQ, D_MODEL)
    kv4 = lambda t: t.reshape(*t.shape[:-1], N_KV_HEADS, HEAD_DIM)
    meta_k_p = jnp.broadcast_to(kv4(k_meta_p)[None], (BATCH, N_META, N_KV_HEADS, HEAD_DIM))
    meta_v_p = jnp.broadcast_to(kv4(v_meta_p)[None], (BATCH, N_META, N_KV_HEADS, HEAD_DIM))
    win_k_s = jnp.concatenate([cache_win_k, kv4(k_new)], axis=1)[:, -WINDOW:]
    win_v_s = jnp.concatenate([cache_win_v, kv4(v_new)], axis=1)[:, -WINDOW:]
    return (y_prompt, y_sample, jnp.stack(conv_p), jnp.stack(conv_s), meta_k_p, meta_v_p,
            kv4(k_fr[:, -WINDOW:]), kv4(v_fr[:, -WINDOW:]), win_k_s, win_v_s)
```

```python
import functools
import math

import jax
import jax.numpy as jnp
from jax import lax
from jax.experimental import pallas as pl
from jax.experimental.pallas import tpu as pltpu

D_MODEL = 2048
BATCH = 8
SEQ = 4096
DEPTH = 4
DEC_BATCH = 8
DEC_SEQ = 16
PAST_LEN = 4096
CHUNK = 64
N_META = 16
N_A = DEPTH // 2
CONV_W = 31
HEAD_DIM = 64
N_HEADS = D_MODEL // HEAD_DIM
N_KV_HEADS = N_HEADS // 8
GROUP = N_HEADS // N_KV_HEADS
WINDOW = 128
WIN_CHUNKS = WINDOW // CHUNK
ROPE_THETA = 10000.0
D_FF = 5632
N_EXPERTS = 8
TOP_K = 2
D_EXPERT = 7168
EPS = 1e-6
NEG_INF = -1e30

F32 = jnp.float32
BF16 = jnp.bfloat16

N_PROMPT = BATCH * SEQ
N_SAMPLE = DEC_BATCH * DEC_SEQ
N_SMALL = 256
M_ROWS = N_PROMPT + N_SMALL
ROW_SAMPLE = N_PROMPT
ROW_META = N_PROMPT + N_SAMPLE

TM = 768
TN = 1024
TF = 512
TB = TM
N_ASSIGN = M_ROWS * TOP_K
N_BLOCKS = N_ASSIGN // TB + N_EXPERTS
TR = 384
SLAB = 256
HIST = 32
CONV_TT = 128
CONV_RG = 128
CONV_LS = 128
FFN_SUBS = (384, 384)
TMR = 384
R_SUB = 192
Q_SUB = 256
CAST_ROWS = 256
KV_W = N_KV_HEADS * HEAD_DIM
KEYS = 256
BAND = (WIN_CHUNKS + 1) * CHUNK
ATT_QB = 8
VMEM_LIMIT = 56 * 1024 * 1024


def _cparams(sem):
    return pltpu.CompilerParams(dimension_semantics=sem, vmem_limit_bytes=VMEM_LIMIT)


def _dot(a, b):
    return jnp.dot(a, b, preferred_element_type=F32)


def _norm_kernel(*refs, n_y, gated, n_g, write_x):
    x_ref = refs[0]
    y_refs = refs[1:1 + n_y]
    pos = 1 + n_y
    gate_ref = refs[pos] if gated else None
    pos += 1 if gated else 0
    g_refs = refs[pos:pos + n_g]
    pos += n_g
    outs = refs[pos:]
    x = x_ref[...]
    i = 0
    for y_ref in y_refs:
        for k in range(y_ref.shape[1] // D_MODEL):
            y = y_ref[:, k * D_MODEL:(k + 1) * D_MODEL]
            x = x + (gate_ref[:, i:i + 1] * y if gated else y)
            i += 1
    o = 0
    if write_x:
        outs[0][...] = x
        o = 1
    if n_g:
        xn = x * lax.rsqrt(jnp.mean(x * x, axis=-1, keepdims=True) + EPS)
        for i, g_ref in enumerate(g_refs):
            outs[o + i][...] = (xn * g_ref[...]).astype(BF16)


def _norm(x, ys=(), gates=None, gains=(), *, tr=TR, row0=0, n_rows=M_ROWS):
    n_y, n_g = len(ys), len(gains)
    write_x = n_y > 0
    blk0 = row0 // tr
    row_in = pl.BlockSpec((tr, D_MODEL), lambda i: (blk0 + i, 0))
    row_out = pl.BlockSpec((tr, D_MODEL), lambda i: (i, 0))
    in_specs = [row_in] + [pl.BlockSpec((tr, y.shape[1]), lambda i: (blk0 + i, 0)) for y in ys]
    args = [x, *ys]
    if gates is not None:
        in_specs.append(pl.BlockSpec((tr, 128), lambda i: (blk0 + i, 0)))
        args.append(gates)
    for g in gains:
        in_specs.append(pl.BlockSpec((1, D_MODEL), lambda i: (0, 0)))
        args.append(g.reshape(1, D_MODEL))
    out_shape, out_specs = [], []
    if write_x:
        out_shape.append(jax.ShapeDtypeStruct((n_rows, D_MODEL), F32))
        out_specs.append(row_out)
    for _ in gains:
        out_shape.append(jax.ShapeDtypeStruct((n_rows, D_MODEL), BF16))
        out_specs.append(row_out)
    return pl.pallas_call(
        functools.partial(_norm_kernel, n_y=n_y, gated=gates is not None, n_g=n_g, write_x=write_x),
        grid=(n_rows // tr,), in_specs=in_specs, out_specs=out_specs, out_shape=out_shape,
        compiler_params=_cparams(("parallel",)), name="norm")(*args)


def _glu_kernel(h_ref, wa_ref, wb_ref, ba_ref, bb_ref, u_ref):
    h = h_ref[...]
    a = _dot(h, wa_ref[0]) + ba_ref[...]
    b = _dot(h, wb_ref[0]) + bb_ref[...]
    u_ref[...] = a * jax.nn.sigmoid(b)


def _glu(h, w, b, layer):
    nj = D_MODEL // TN
    b2 = b.reshape(1, 2 * D_MODEL)
    return pl.pallas_call(
        _glu_kernel, grid=(nj, M_ROWS // TM),
        in_specs=[pl.BlockSpec((TM, D_MODEL), lambda j, i: (i, 0)),
                  pl.BlockSpec((1, D_MODEL, TN), lambda j, i: (layer, 0, j)),
                  pl.BlockSpec((1, D_MODEL, TN), lambda j, i: (layer, 0, j + nj)),
                  pl.BlockSpec((1, TN), lambda j, i: (0, j)),
                  pl.BlockSpec((1, TN), lambda j, i: (0, j + nj))],
        out_specs=pl.BlockSpec((TM, TN), lambda j, i: (i, j)),
        out_shape=jax.ShapeDtypeStruct((M_ROWS, D_MODEL), F32),
        compiler_params=_cparams(("parallel", "parallel")), name="pw1_glu")(h, w, w, b2, b2)


def _top2(h, wr_ref, idx_ref, gate_ref, rows):
    h_hi = h.astype(BF16)
    h_lo = (h - h_hi.astype(F32)).astype(BF16)
    logits = _dot(h_hi, wr_ref[0]) + (_dot(h_hi, wr_ref[1]) + _dot(h_lo, wr_ref[0]))
    lane_i = lax.broadcasted_iota(jnp.int32, logits.shape, 1)
    lane = lane_i.astype(F32)
    logits = jnp.where(lane_i < N_EXPERTS, logits, -jnp.inf)
    m1 = jnp.max(logits, axis=-1, keepdims=True)
    i1 = jnp.min(jnp.where(logits == m1, lane, 128.0), axis=-1, keepdims=True)
    rest = jnp.where(lane == i1, -jnp.inf, logits)
    m2 = jnp.max(rest, axis=-1, keepdims=True)
    i2 = jnp.min(jnp.where(rest == m2, lane, 128.0), axis=-1, keepdims=True)
    e2 = jnp.exp(m2 - m1)
    g1 = 1.0 / (1.0 + e2)
    idx_ref[rows, :] = jnp.where(lane_i == 0, i1, i2).astype(jnp.int32)
    gate_ref[rows, :] = jnp.where(lane_i == 0, g1, e2 * g1)


def _resid_kernel(*refs, routed, n_after):
    n_in = 6 if routed else 5
    refs = refs[:n_in] + refs[n_in + n_after:]
    if routed:
        x_ref, w_ref, b_ref, r_ref, g_ref, wr_ref, xo_ref, h_ref, idx_ref, gate_ref = refs
    else:
        x_ref, w_ref, b_ref, r_ref, g_ref, xo_ref, h_ref = refs
    for s in range(TMR // R_SUB):
        rows = slice(s * R_SUB, (s + 1) * R_SUB)
        xn = r_ref[rows, :] + (_dot(x_ref[rows, :], w_ref[0]) + b_ref[...])
        xo_ref[rows, :] = xn
        h = xn * lax.rsqrt(jnp.mean(xn * xn, axis=-1, keepdims=True) + EPS) * g_ref[...]
        h_ref[rows, :] = h.astype(BF16)
        if routed:
            _top2(h, wr_ref, idx_ref, gate_ref, rows)


def _resid_mm(x, w, b, r, layer, gain, w_router=None, after=()):
    routed = w_router is not None
    row = lambda width: pl.BlockSpec((TMR, width), lambda i: (i, 0))
    vec = pl.BlockSpec((1, D_MODEL), lambda i: (0, 0))
    in_specs = [row(D_MODEL), pl.BlockSpec((1, D_MODEL, D_MODEL), lambda i: (layer, 0, 0)), vec, row(D_MODEL), vec]
    args = [x, w, b.reshape(1, D_MODEL), r, gain.reshape(1, D_MODEL)]
    out_specs = [row(D_MODEL), row(D_MODEL)]
    out_shape = [jax.ShapeDtypeStruct((M_ROWS, D_MODEL), F32), jax.ShapeDtypeStruct((M_ROWS, D_MODEL), BF16)]
    if routed:
        in_specs.append(pl.BlockSpec((2, D_MODEL, 128), lambda i: (0, 0, 0)))
        wr = jnp.zeros((D_MODEL, 128), F32).at[:, :N_EXPERTS].set(w_router)
        wr_hi = wr.astype(BF16)
        args.append(jnp.stack([wr_hi, (wr - wr_hi.astype(F32)).astype(BF16)]))
        out_specs += [row(128), row(128)]
        out_shape += [jax.ShapeDtypeStruct((M_ROWS, 128), jnp.int32), jax.ShapeDtypeStruct((M_ROWS, 128), F32)]
    in_specs += [pl.BlockSpec(memory_space=pl.ANY)] * len(after)
    args += list(after)
    return pl.pallas_call(
        functools.partial(_resid_kernel, routed=routed, n_after=len(after)), grid=(M_ROWS // TMR,),
        in_specs=in_specs, out_specs=out_specs, out_shape=out_shape,
        compiler_params=_cparams(("parallel",)), name="proj_resid_norm")(*args)


def _head_norm_rope(x, gain, cos, sin, bd):
    ssq = _dot((x * x).astype(BF16), bd)
    xn = x * lax.rsqrt(ssq * (1.0 / HEAD_DIM) + EPS) * gain
    lane = lax.broadcasted_iota(jnp.int32, xn.shape, 1)
    first_half = (lane % HEAD_DIM) < (HEAD_DIM // 2)
    rot = jnp.where(first_half, pltpu.roll(xn, SLAB - HEAD_DIM // 2, 1), pltpu.roll(xn, HEAD_DIM // 2, 1))
    return xn * cos + rot * sin


def _q_kernel(h_ref, w_ref, g_ref, cos_ref, sin_ref, bd_ref, q_ref):
    bd, gain = bd_ref[...], g_ref[...]
    scale = HEAD_DIM ** -0.5
    for r in range(TM // Q_SUB):
        rows = slice(r * Q_SUB, (r + 1) * Q_SUB)
        acc = _dot(h_ref[rows, :], w_ref[0])
        cos, sin = cos_ref[rows, :], sin_ref[rows, :]
        for s in range(TN // SLAB):
            q = _head_norm_rope(acc[:, s * SLAB:(s + 1) * SLAB], gain, cos, sin, bd)
            q_ref[rows, s * SLAB:(s + 1) * SLAB] = (q * scale).astype(BF16)


def _queries(h, w, g_q, cos, sin, bd, layer):
    return pl.pallas_call(
        _q_kernel, grid=(D_MODEL // TN, M_ROWS // TM),
        in_specs=[pl.BlockSpec((TM, D_MODEL), lambda j, i: (i, 0)),
                  pl.BlockSpec((1, D_MODEL, TN), lambda j, i: (layer, 0, j)),
                  pl.BlockSpec((1, SLAB), lambda j, i: (0, 0)),
                  pl.BlockSpec((TM, SLAB), lambda j, i: (i, 0)),
                  pl.BlockSpec((TM, SLAB), lambda j, i: (i, 0)),
                  pl.BlockSpec((SLAB, SLAB), lambda j, i: (0, 0))],
        out_specs=pl.BlockSpec((TM, TN), lambda j, i: (i, j)),
        out_shape=jax.ShapeDtypeStruct((M_ROWS, D_MODEL), BF16),
        compiler_params=_cparams(("parallel", "parallel")), name="q_proj")(
            h, w, jnp.tile(g_q, SLAB // HEAD_DIM).reshape(1, SLAB), cos, sin, bd)


def _kv_kernel(h_ref, w_ref, g_ref, cos_ref, sin_ref, bd_ref, k_ref, v_ref):
    acc = _dot(h_ref[...], w_ref[...])
    k_ref[...] = _head_norm_rope(acc[:, :KV_W], g_ref[...], cos_ref[...], sin_ref[...], bd_ref[...])
    v_ref[...] = acc[:, KV_W:]


def _shared_kv(h, w, g_k, cos, sin, bd):
    return pl.pallas_call(
        _kv_kernel, grid=(M_ROWS // TM,),
        in_specs=[pl.BlockSpec((TM, D_MODEL), lambda i: (i, 0)),
                  pl.BlockSpec((D_MODEL, 2 * KV_W), lambda i: (0, 0)),
                  pl.BlockSpec((1, SLAB), lambda i: (0, 0)),
                  pl.BlockSpec((TM, SLAB), lambda i: (i, 0)),
                  pl.BlockSpec((TM, SLAB), lambda i: (i, 0)),
                  pl.BlockSpec((SLAB, SLAB), lambda i: (0, 0))],
        out_specs=[pl.BlockSpec((TM, KV_W), lambda i: (i, 0))] * 2,
        out_shape=[jax.ShapeDtypeStruct((M_ROWS, KV_W), F32)] * 2,
        compiler_params=_cparams(("parallel",)), name="kv_proj")(
            h, w, jnp.tile(g_k, SLAB // HEAD_DIM).reshape(1, SLAB), cos, sin, bd)


def _conv_kernel(*refs, tt, use_prev, has_dst):
    c_ref, ext_ref, acc_ref = refs[-3:]
    ins = refs[:-4] if has_dst else refs[:-3]
    if use_prev:
        u_ref, prev_ref, hist_ref, w_ref, bdw_ref, gln_ref, bln_ref = ins
    else:
        u_ref, hist_ref, w_ref, bdw_ref, gln_ref, bln_ref = ins
    if use_prev:
        first = pl.program_id(1) == 0

        @pl.when(first)
        def _():
            ext_ref[0:HIST, :] = hist_ref[0]

        @pl.when(jnp.logical_not(first))
        def _():
            ext_ref[0:HIST, :] = prev_ref[...]
    else:
        ext_ref[0:HIST, :] = hist_ref[0]
    ext_ref[HIST:HIST + tt, :] = u_ref[...]
    rg = min(CONV_RG, tt)
    off = HIST - (CONV_W - 1)
    for r in range(tt // rg):
        for s in range(D_MODEL // CONV_LS):
            ls = slice(s * CONV_LS, (s + 1) * CONV_LS)
            acc = jnp.zeros((rg, CONV_LS), F32)
            for res in range(8):
                offs = [o for o in range(off, off + CONV_W) if o % 8 == res]
                win = ext_ref[r * rg + offs[0]:r * rg + offs[-1] + rg, ls]
                part = None
                for o in offs:
                    k = o - off
                    term = win[o - offs[0]:o - offs[0] + rg, :] * w_ref[k:k + 1, ls]
                    part = term if part is None else part + term
                acc = acc + part
            acc_ref[r * rg:(r + 1) * rg, ls] = acc
    x = acc_ref[...] + bdw_ref[...]
    xc = x - jnp.mean(x, axis=-1, keepdims=True)
    y = xc * lax.rsqrt(jnp.mean(xc * xc, axis=-1, keepdims=True) + EPS)
    y = y * gln_ref[...] + bln_ref[...]
    c_ref[...] = (y * jax.nn.sigmoid(y)).astype(BF16)


def _conv(u, hist, w_dw, b_dw, g_ln, b_ln, *, row0, n_seq, seq_len, tt, dst=None):
    n_t = seq_len // tt
    use_prev = n_t > 1
    blk0 = row0 // tt
    wpad = jnp.zeros((32, D_MODEL), F32).at[:CONV_W].set(w_dw)
    vec = pl.BlockSpec((1, D_MODEL), lambda s, i: (0, 0))
    in_specs = [pl.BlockSpec((tt, D_MODEL), lambda s, i: (blk0 + s * n_t + i, 0))]
    args = [u]
    if use_prev:
        per = tt // HIST
        in_specs.append(pl.BlockSpec(
            (HIST, D_MODEL), lambda s, i: (jnp.maximum((blk0 + s * n_t + i) * per - 1, 0), 0)))
        args.append(u)
    in_specs += [pl.BlockSpec((1, HIST, D_MODEL), lambda s, i: (s, 0, 0)),
                 pl.BlockSpec((32, D_MODEL), lambda s, i: (0, 0)), vec, vec, vec]
    args += [hist, wpad, b_dw.reshape(1, D_MODEL), g_ln.reshape(1, D_MODEL), b_ln.reshape(1, D_MODEL)]
    aliases = {}
    if dst is not None:
        aliases = {len(args): 0}
        in_specs.append(pl.BlockSpec(memory_space=pl.ANY))
        args.append(dst)
    return pl.pallas_call(
        functools.partial(_conv_kernel, tt=tt, use_prev=use_prev, has_dst=dst is not None), grid=(n_seq, n_t),
        in_specs=in_specs,
        out_specs=pl.BlockSpec((tt, D_MODEL), lambda s, i: (blk0 + s * n_t + i, 0)),
        out_shape=jax.ShapeDtypeStruct((M_ROWS, D_MODEL), BF16),
        input_output_aliases=aliases,
        scratch_shapes=[pltpu.VMEM((HIST + tt, D_MODEL), F32), pltpu.VMEM((tt, D_MODEL), F32)],
        compiler_params=_cparams(("parallel", "arbitrary")), name="conv_ln_swish")(*args)


def _ffn_kernel(be_ref, nv_ref, x_ref, wg_ref, wu_ref, wd_ref, *rest):
    o_ref, acc_ref = rest[-2:]
    i, f = pl.program_id(0), pl.program_id(1)
    n_valid = nv_ref[i]

    @pl.when(jnp.logical_and(f == 0, n_valid <= FFN_SUBS[0]))
    def _():
        acc_ref[...] = jnp.zeros_like(acc_ref)

    def run(subs):
        r0 = 0
        for n in subs:
            rows = slice(r0, r0 + n)
            r0 += n
            x = x_ref[rows, :]
            g = _dot(x, wg_ref[0, 0, 0])
            u = _dot(x, wu_ref[0, 0, 0])
            a = (g * jax.nn.sigmoid(g) * u).astype(BF16)
            acc_ref[rows, :] = jnp.where(f == 0, 0.0, acc_ref[rows, :]) + _dot(a, wd_ref[0, 0])

    @pl.when(n_valid > FFN_SUBS[0])
    def _():
        run(FFN_SUBS)

    @pl.when(jnp.logical_and(n_valid > 0, n_valid <= FFN_SUBS[0]))
    def _():
        run(FFN_SUBS[:1])

    @pl.when(f == pl.num_programs(1) - 1)
    def _():
        o_ref[...] = acc_ref[...].astype(BF16)


def _ffn(x, block_expert, block_rows, wg, wu, wd, layer, *, out_blocks=None, blk0=0, dst=None):
    n_blocks = x.shape[0] // TB
    out_blocks = n_blocks if out_blocks is None else out_blocks
    n_f = wg.shape[2]

    def f_idx(i, f, ok):
        return jnp.where(ok[i] > 0, f, n_f - 1)

    in_specs = [pl.BlockSpec((TB, D_MODEL), lambda i, f, be, ok: (i, 0)),
                pl.BlockSpec((1, 1, 1, D_MODEL, TF),
                             lambda i, f, be, ok: (layer, be[i], f_idx(i, f, ok), 0, 0)),
                pl.BlockSpec((1, 1, 1, D_MODEL, TF),
                             lambda i, f, be, ok: (layer, be[i], f_idx(i, f, ok), 0, 0)),
                pl.BlockSpec((1, 1, TF, D_MODEL), lambda i, f, be, ok: (layer, be[i], f_idx(i, f, ok), 0))]
    args = [block_expert, block_rows, x, wg, wu, wd]
    aliases = {}
    if dst is not None:
        aliases = {len(args): 0}
        in_specs.append(pl.BlockSpec(memory_space=pl.ANY))
        args.append(dst)
    grid_spec = pltpu.PrefetchScalarGridSpec(
        num_scalar_prefetch=2, grid=(n_blocks, n_f), in_specs=in_specs,
        out_specs=pl.BlockSpec((TB, D_MODEL), lambda i, f, be, ok: (blk0 + i, 0)),
        scratch_shapes=[pltpu.VMEM((TB, D_MODEL), F32)])
    return pl.pallas_call(
        _ffn_kernel, grid_spec=grid_spec,
        out_shape=jax.ShapeDtypeStruct((out_blocks * TB, D_MODEL), BF16), input_output_aliases=aliases,
        compiler_params=_cparams(("parallel", "arbitrary")), name="swiglu")(*args)


def _moe(h, idx, wg, wu, wd, layer):
    e_flat = idx[:, :TOP_K].reshape(N_ASSIGN)
    onehot = (e_flat[:, None] == jnp.arange(N_EXPERTS, dtype=jnp.int32)[None, :]).astype(jnp.int32)
    csum = jnp.cumsum(onehot, axis=0)
    rank = jnp.sum(csum * onehot, axis=-1) - 1
    counts = csum[-1]
    padded = (counts + TB - 1) // TB * TB
    pad_ends = jnp.cumsum(padded)
    pad_starts = pad_ends - padded
    dest = pad_starts[e_flat] + rank
    blk_start = jnp.arange(N_BLOCKS, dtype=jnp.int32) * TB
    block_expert = jnp.minimum(jnp.searchsorted(pad_ends, blk_start, side='right'),
                               N_EXPERTS - 1).astype(jnp.int32)
    real_end = pad_starts + counts
    block_rows = jnp.where(blk_start < pad_ends[-1],
                           jnp.clip(real_end[block_expert] - blk_start, 0, TB), 0).astype(jnp.int32)
    src = jnp.zeros((N_BLOCKS * TB,), jnp.int32).at[dest].set(jnp.arange(N_ASSIGN, dtype=jnp.int32) // TOP_K)
    half = N_BLOCKS // 2
    y_buf = _ffn(h[src[:half * TB]], block_expert[:half], block_rows[:half], wg, wu, wd, layer,
                 out_blocks=N_BLOCKS)
    y_buf = _ffn(h[src[half * TB:]], block_expert[half:], block_rows[half:], wg, wu, wd, layer,
                 out_blocks=N_BLOCKS, blk0=half, dst=y_buf)
    return y_buf[dest].reshape(M_ROWS, TOP_K * D_MODEL)


def _attn_kernel(*refs, tq, qb, hi, prompt):
    q_ref, k_ref, v_ref, km_ref, vm_ref, sink_ref = refs[:6]
    o_ref = refs[-1]
    _attn_body(q_ref, k_ref, v_ref, km_ref, vm_ref, sink_ref, o_ref, tq=tq, qb=qb, hi=hi, prompt=prompt)


def _attn_body(q_ref, k_ref, v_ref, km_ref, vm_ref, sink_ref, o_ref, *, tq, qb, hi, prompt):
    lane = lax.broadcasted_iota(jnp.int32, (KEYS, 128), 1)
    low = lane < HEAD_DIM
    key = lax.broadcasted_iota(jnp.int32, (1, 2 * KEYS), 1) % KEYS
    is_meta = jnp.logical_and(key >= BAND, key < BAND + N_META)
    zpad_k = jnp.zeros((KEYS - BAND - N_META, 128), BF16)
    ones_row = lax.broadcasted_iota(jnp.int32, (2 * KEYS, 128), 0) < KEYS
    ones_low = lax.broadcasted_iota(jnp.int32, (2 * KEYS, 128), 1) < HEAD_DIM
    sum_cols = jnp.where(ones_row == ones_low, 1.0, 0.0).astype(BF16)
    out_low = lax.broadcasted_iota(jnp.int32, (4 * tq, 128), 1) < HEAD_DIM

    def chunk(cc, carry):
        c_glob = pl.program_id(1) * qb + cc
        row0 = pl.multiple_of(cc * tq, tq)
        key0 = pl.multiple_of(c_glob * CHUNK, CHUNK) if prompt else 0
        lo = jnp.maximum(WIN_CHUNKS - c_glob, 0) * CHUNK if prompt else 0
        ok = jnp.logical_or(jnp.logical_and(key >= lo, key < hi), is_meta)
        bias = jnp.where(ok, 0.0, NEG_INF)
        for g in range(N_KV_HEADS):
            gl = slice(g * 128, (g + 1) * 128)
            k_all = jnp.concatenate([k_ref[0, pl.ds(key0, BAND), gl], km_ref[0, :, gl], zpad_k], axis=0)
            v_all = jnp.concatenate([v_ref[0, pl.ds(key0, BAND), gl], vm_ref[0, :, gl], zpad_k], axis=0)
            k2 = jnp.concatenate([jnp.where(low, k_all, 0), jnp.where(low, 0, k_all)], axis=0)
            v2 = jnp.concatenate([jnp.where(low, v_all, 0), jnp.where(low, 0, v_all)], axis=0)
            qp = jnp.concatenate(
                [q_ref[pl.ds(row0, tq), g * 512 + i * 128:g * 512 + (i + 1) * 128] for i in range(4)], axis=0)
            s = lax.dot_general(qp, k2, (((1,), (1,)), ((), ())), preferred_element_type=F32) + bias
            es, sink_e = [], []
            for p in range(2):
                sp = s[:, p * KEYS:(p + 1) * KEYS]
                sk = sink_ref[g, :, p:p + 1]
                m = jnp.maximum(jnp.max(sp, axis=-1, keepdims=True), sk)
                es.append(jnp.exp(sp - m).astype(BF16))
                sink_e.append(jnp.exp(sk - m))
            ox = _dot(jnp.concatenate(es, axis=1), jnp.concatenate([v2, sum_cols], axis=1))
            den = ox[:, 128:] + jnp.where(out_low, sink_e[0], sink_e[1])
            o = ox[:, :128] * (1.0 / den)
            for i in range(4):
                o_ref[pl.ds(row0, tq), g * 512 + i * 128:g * 512 + (i + 1) * 128] = (
                    o[i * tq:(i + 1) * tq, :].astype(BF16))
        return carry

    lax.fori_loop(0, qb, chunk, 0, unroll=min(4, qb))


def _attention(q, kd, vd, kmd, vmd, sinks, *, row0, n_b, n_rows, tq, qb, hi, prompt, dst=None):
    blk_rows = tq * qb
    n_i = n_rows // blk_rows
    blk0 = row0 // blk_rows
    per_b_meta = kmd.shape[0] > 1
    sk = sinks.reshape(N_KV_HEADS, 4, 1, 2)
    sk = jnp.broadcast_to(sk, (N_KV_HEADS, 4, tq, 2)).reshape(N_KV_HEADS, 4 * tq, 2)
    n_keys = kd.shape[1]
    kv_spec = pl.BlockSpec((1, n_keys, 512), lambda b, i: (b, 0, 0))
    meta_spec = pl.BlockSpec((1, N_META, 512), lambda b, i: (b if per_b_meta else 0, 0, 0))
    row_spec = pl.BlockSpec((blk_rows, D_MODEL), lambda b, i: (blk0 + b * n_i + i, 0))
    in_specs = [row_spec, kv_spec, kv_spec, meta_spec, meta_spec,
                pl.BlockSpec((N_KV_HEADS, 4 * tq, 2), lambda b, i: (0, 0, 0))]
    args = [q, kd, vd, kmd, vmd, sk]
    aliases = {}
    if dst is not None:
        aliases = {len(args): 0}
        in_specs.append(pl.BlockSpec(memory_space=pl.ANY))
        args.append(dst)
    return pl.pallas_call(
        functools.partial(_attn_kernel, tq=tq, qb=qb, hi=hi, prompt=prompt),
        grid=(n_b, n_i), in_specs=in_specs, out_specs=row_spec,
        out_shape=jax.ShapeDtypeStruct((M_ROWS, D_MODEL), BF16), input_output_aliases=aliases,
        compiler_params=_cparams(("parallel", "arbitrary")), name="swa")(*args)


def _cast_kernel(w_ref, o_ref):
    o_ref[...] = w_ref[...].astype(BF16)


def _cast_rows(w):
    n_l, n_e, d_hidden, d_out = w.shape
    spec = pl.BlockSpec((1, TF, d_out), lambda e, i: (e, i, 0))
    out = pl.pallas_call(
        _cast_kernel, grid=(n_l * n_e, d_hidden // TF), in_specs=[spec], out_specs=spec,
        out_shape=jax.ShapeDtypeStruct((n_l * n_e, d_hidden, d_out), BF16),
        compiler_params=_cparams(("parallel", "parallel")), name="cast")(w.reshape(n_l * n_e, d_hidden, d_out))
    return out.reshape(n_l, n_e, d_hidden, d_out)


def _tile_cols_kernel(w_ref, o_ref):
    for f in range(o_ref.shape[1]):
        o_ref[0, f] = w_ref[0, :, f * TF:(f + 1) * TF].astype(BF16)


def _tile_cols(w):
    n_l, n_e, d_in, d_hidden = w.shape
    n_f = d_hidden // TF
    out = pl.pallas_call(
        _tile_cols_kernel, grid=(n_l * n_e, d_in // CAST_ROWS),
        in_specs=[pl.BlockSpec((1, CAST_ROWS, d_hidden), lambda e, i: (e, i, 0))],
        out_specs=pl.BlockSpec((1, n_f, CAST_ROWS, TF), lambda e, i: (e, 0, i, 0)),
        out_shape=jax.ShapeDtypeStruct((n_l * n_e, n_f, d_in, TF), BF16),
        compiler_params=_cparams(("parallel", "parallel")), name="cast_tile")(w.reshape(n_l * n_e, d_in, d_hidden))
    return out.reshape(n_l, n_e, n_f, d_in, TF)


def _dup_heads(x):
    lead = x.shape[:-1]
    x = x.astype(BF16).reshape(*lead, N_KV_HEADS, 1, HEAD_DIM)
    return jnp.broadcast_to(x, (*lead, N_KV_HEADS, 2, HEAD_DIM)).reshape(*lead, 2 * KV_W)


def kernel(x_prompt, x_sample, state_conv, cache_meta_k, cache_meta_v, cache_win_k, cache_win_v, meta_tokens, g_conv, w_pw1, b_pw1, w_dw, b_dw, g_conv_ln, b_conv_ln, w_pw2, b_pw2, g_kv, w_kv, g_k, g_attn, w_q, g_q, sinks, w_o, g_ffn, w_ffn_gate, w_ffn_up, w_ffn_down, g_moe, w_router, w_moe_gate, w_moe_up, w_moe_down):
    bf = lambda w: w.astype(BF16)
    x = jnp.concatenate([x_prompt.reshape(N_PROMPT, D_MODEL), x_sample.reshape(N_SAMPLE, D_MODEL),
                         meta_tokens, jnp.zeros((M_ROWS - ROW_META - N_META, D_MODEL), F32)], axis=0)

    pos = jnp.concatenate([jnp.tile(N_META + jnp.arange(SEQ), BATCH),
                           jnp.tile(N_META + PAST_LEN + jnp.arange(DEC_SEQ), DEC_BATCH),
                           jnp.arange(N_META), jnp.zeros((M_ROWS - ROW_META - N_META,), jnp.int32)])
    half = HEAD_DIM // 2
    inv = jnp.exp(jnp.arange(half, dtype=F32) * (-2.0 * math.log(ROPE_THETA) / HEAD_DIM))
    ang = pos.astype(F32)[:, None] * inv[None, :]
    cos_t = jnp.tile(jnp.concatenate([jnp.cos(ang), jnp.cos(ang)], axis=-1), (1, SLAB // HEAD_DIM))
    sin_t = jnp.tile(jnp.concatenate([-jnp.sin(ang), jnp.sin(ang)], axis=-1), (1, SLAB // HEAD_DIM))
    hd = jnp.arange(SLAB) // HEAD_DIM
    bd = (hd[:, None] == hd[None, :]).astype(BF16)

    dense_expert = jnp.zeros((M_ROWS // TB,), jnp.int32)
    dense_rows = jnp.full((M_ROWS // TB,), TB, jnp.int32)
    conv_p, conv_s = [], []
    w_pw1, w_pw2, w_q, w_o, w_kv = bf(w_pw1), bf(w_pw2), bf(w_q), bf(w_o), bf(w_kv)
    w_ffn_gate, w_ffn_up = _tile_cols(w_ffn_gate[:, None]), _tile_cols(w_ffn_up[:, None])
    w_moe_gate, w_moe_up = _tile_cols(w_moe_gate), _tile_cols(w_moe_up)
    w_ffn_down, w_moe_down = _cast_rows(w_ffn_down[:, None]), _cast_rows(w_moe_down)

    (h,) = _norm(x, gains=(g_conv[0],))
    for l in range(N_A):
        u = _glu(h, w_pw1, b_pw1[l], l)
        u_meta = u[ROW_META:ROW_META + N_META]
        u_samp = u[ROW_SAMPLE:ROW_SAMPLE + N_SAMPLE].reshape(DEC_BATCH, DEC_SEQ, D_MODEL)
        hist_p = jnp.broadcast_to(
            jnp.concatenate([jnp.zeros((HIST - N_META, D_MODEL), F32), u_meta], axis=0)[None],
            (BATCH, HIST, D_MODEL))
        hist_s = jnp.zeros((N_SMALL // DEC_SEQ, HIST, D_MODEL), F32).at[:DEC_BATCH, HIST - (CONV_W - 1):].set(
            state_conv[l])
        conv_args = (w_dw[l], b_dw[l], g_conv_ln[l], b_conv_ln[l])
        c = _conv(u, hist_p, *conv_args, row0=0, n_seq=BATCH, seq_len=SEQ, tt=CONV_TT)
        c = _conv(u, hist_s, *conv_args, row0=ROW_SAMPLE, n_seq=N_SMALL // DEC_SEQ, seq_len=DEC_SEQ,
                  tt=DEC_SEQ, dst=c)
        conv_p.append(jnp.stack([u[(b + 1) * SEQ - (CONV_W - 1):(b + 1) * SEQ] for b in range(BATCH)]))
        conv_s.append(jnp.concatenate([state_conv[l], u_samp], axis=1)[:, -(CONV_W - 1):])
        if l == 0:
            x, h = _resid_mm(c, w_pw2, b_pw2[l], x, l, g_ffn[0])
            y = _ffn(h, dense_expert, dense_rows, w_ffn_gate, w_ffn_up, w_ffn_down, 0)
            x, h = _norm(x, ys=(y,), gains=(g_conv[1],))
        else:
            x, h, idx, gates = _resid_mm(c, w_pw2, b_pw2[l], x, l, g_moe[0], w_router[0],
                                         after=(w_moe_gate, w_moe_up, w_moe_down))
            yy = _moe(h, idx, w_moe_gate, w_moe_up, w_moe_down, 0)
            x, h_kv, h = _norm(x, ys=(yy,), gates=gates, gains=(g_kv, g_attn[0]))

    k, v = _shared_kv(h_kv, w_kv, g_k, cos_t, sin_t, bd)
    k_fr = k[:N_PROMPT].reshape(BATCH, SEQ, KV_W)
    v_fr = v[:N_PROMPT].reshape(BATCH, SEQ, KV_W)
    k_new = k[ROW_SAMPLE:ROW_SAMPLE + N_SAMPLE].reshape(DEC_BATCH, DEC_SEQ, KV_W)
    v_new = v[ROW_SAMPLE:ROW_SAMPLE + N_SAMPLE].reshape(DEC_BATCH, DEC_SEQ, KV_W)
    k_meta_p = k[ROW_META:ROW_META + N_META]
    v_meta_p = v[ROW_META:ROW_META + N_META]
    front = jnp.zeros((BATCH, WINDOW, KV_W), F32)
    kd_p = _dup_heads(jnp.concatenate([front, k_fr], axis=1))
    vd_p = _dup_heads(jnp.concatenate([front, v_fr], axis=1))
    tail = jnp.zeros((DEC_BATCH, BAND - WINDOW - DEC_SEQ, KV_W), F32)
    pad_b = lambda t: jnp.concatenate(
        [t, jnp.zeros((N_SMALL // DEC_SEQ - DEC_BATCH, *t.shape[1:]), t.dtype)], axis=0)
    kd_s = _dup_heads(pad_b(jnp.concatenate([cache_win_k.reshape(DEC_BATCH, WINDOW, KV_W), k_new, tail], axis=1)))
    vd_s = _dup_heads(pad_b(jnp.concatenate([cache_win_v.reshape(DEC_BATCH, WINDOW, KV_W), v_new, tail], axis=1)))
    kmd_p, vmd_p = _dup_heads(k_meta_p)[None], _dup_heads(v_meta_p)[None]
    kmd_s = _dup_heads(pad_b(cache_meta_k.reshape(DEC_BATCH, N_META, KV_W)))
    vmd_s = _dup_heads(pad_b(cache_meta_v.reshape(DEC_BATCH, N_META, KV_W)))

    for j in range(DEPTH - N_A):
        q = _queries(h, w_q, g_q[j], cos_t, sin_t, bd, j)
        o = _attention(q, kd_p, vd_p, kmd_p, vmd_p, sinks[j], row0=0, n_b=BATCH, n_rows=SEQ, tq=CHUNK,
                       qb=ATT_QB, hi=BAND, prompt=True)
        o = _attention(q, kd_s, vd_s, kmd_s, vmd_s, sinks[j], row0=ROW_SAMPLE, n_b=N_SMALL // DEC_SEQ,
                       n_rows=DEC_SEQ, tq=DEC_SEQ, qb=1, hi=WINDOW + DEC_SEQ, prompt=False, dst=o)
        no_bias = jnp.zeros((D_MODEL,), F32)
        if j == 0:
            x, h = _resid_mm(o, w_o, no_bias, x, j, g_ffn[1])
            y = _ffn(h, dense_expert, dense_rows, w_ffn_gate, w_ffn_up, w_ffn_down, 1)
            x, h = _norm(x, ys=(y,), gains=(g_attn[1],))
        else:
            x, h, idx, gates = _resid_mm(o, w_o, no_bias, x, j, g_moe[1], w_router[1])
            yy = _moe(h, idx, w_moe_gate, w_moe_up, w_moe_down, 1)
            (y_prompt,) = _norm(x, ys=(yy,), gates=gates, tr=256, row0=0, n_rows=N_PROMPT)
            (y_sample,) = _norm(x, ys=(yy,), gates=gates, tr=N_SAMPLE, row0=ROW_SAMPLE, n_rows=N_SAMPLE)

    y_prompt = y_prompt.reshape(BATCH, SEQ, D_MODEL)
    y_sample = y_sample.reshape(DEC_BATCH, DEC_SEQ, D_MODEL)
    kv4 = lambda t: t.reshape(*t.shape[:-1], N_KV_HEADS, HEAD_DIM)
    meta_k_p = jnp.broadcast_to(kv4(k_meta_p)[None], (BATCH, N_META, N_KV_HEADS, HEAD_DIM))
    meta_v_p = jnp.broadcast_to(kv4(v_meta_p)[None], (BATCH, N_META, N_KV_HEADS, HEAD_DIM))
    win_k_s = jnp.concatenate([cache_win_k, kv4(k_new)], axis=1)[:, -WINDOW:]
    win_v_s = jnp.concatenate([cache_win_v, kv4(v_new)], axis=1)[:, -WINDOW:]
    return (y_prompt, y_sample, jnp.stack(conv_p), jnp.stack(conv_s), meta_k_p, meta_v_p,
            kv4(k_fr[:, -WINDOW:]), kv4(v_fr[:, -WINDOW:]), win_k_s, win_v_s)
```

```python
import functools
import math

import jax
import jax.numpy as jnp
from jax import lax
from jax.experimental import pallas as pl
from jax.experimental.pallas import tpu as pltpu

D_MODEL = 2048
BATCH = 8
SEQ = 4096
DEPTH = 4
DEC_BATCH = 8
DEC_SEQ = 16
PAST_LEN = 4096
CHUNK = 64
N_META = 16
N_A = DEPTH // 2
CONV_W = 31
HEAD_DIM = 64
N_HEADS = D_MODEL // HEAD_DIM
N_KV_HEADS = N_HEADS // 8
GROUP = N_HEADS // N_KV_HEADS
WINDOW = 128
WIN_CHUNKS = WINDOW // CHUNK
ROPE_THETA = 10000.0
D_FF = 5632
N_EXPERTS = 8
TOP_K = 2
D_EXPERT = 7168
EPS = 1e-6
NEG_INF = -1e30

F32 = jnp.float32
BF16 = jnp.bfloat16

N_PROMPT = BATCH * SEQ
N_SAMPLE = DEC_BATCH * DEC_SEQ
N_SMALL = 256
M_ROWS = N_PROMPT + N_SMALL
ROW_SAMPLE = N_PROMPT
ROW_META = N_PROMPT + N_SAMPLE

TM = 768
TN = 1024
TF = 512
TB = TM
N_ASSIGN = M_ROWS * TOP_K
N_BLOCKS = N_ASSIGN // TB + N_EXPERTS
TR = 384
SLAB = 256
HIST = 32
CONV_TT = 128
CONV_RG = 128
CONV_LS = 128
FFN_SUBS = (384, 384)
TMR = 384
R_SUB = 192
Q_SUB = 256
CAST_ROWS = 256
KV_W = N_KV_HEADS * HEAD_DIM
KEYS = 256
BAND = (WIN_CHUNKS + 1) * CHUNK
ATT_QB = 8
VMEM_LIMIT = 56 * 1024 * 1024


def _cparams(sem):
    return pltpu.CompilerParams(dimension_semantics=sem, vmem_limit_bytes=VMEM_LIMIT)


def _dot(a, b):
    return jnp.dot(a, b, preferred_element_type=F32)


def _norm_kernel(*refs, n_y, gated, n_g, write_x):
    x_ref = refs[0]
    y_refs = refs[1:1 + n_y]
    pos = 1 + n_y
    gate_ref = refs[pos] if gated else None
    pos += 1 if gated else 0
    g_refs = refs[pos:pos + n_g]
    pos += n_g
    outs = refs[pos:]
    x = x_ref[...]
    for i, y_ref in enumerate(y_refs):
        y = y_ref[...]
        x = x + (gate_ref[:, i:i + 1] * y if gated else y)
    o = 0
    if write_x:
        outs[0][...] = x
        o = 1
    if n_g:
        xn = x * lax.rsqrt(jnp.mean(x * x, axis=-1, keepdims=True) + EPS)
        for i, g_ref in enumerate(g_refs):
            outs[o + i][...] = (xn * g_ref[...]).astype(BF16)


def _norm(x, ys=(), gates=None, gains=(), *, tr=TR, row0=0, n_rows=M_ROWS):
    n_y, n_g = len(ys), len(gains)
    write_x = n_y > 0
    blk0 = row0 // tr
    row_in = pl.BlockSpec((tr, D_MODEL), lambda i: (blk0 + i, 0))
    row_out = pl.BlockSpec((tr, D_MODEL), lambda i: (i, 0))
    in_specs = [row_in] * (1 + n_y)
    args = [x, *ys]
    if gates is not None:
        in_specs.append(pl.BlockSpec((tr, 128), lambda i: (blk0 + i, 0)))
        args.append(gates)
    for g in gains:
        in_specs.append(pl.BlockSpec((1, D_MODEL), lambda i: (0, 0)))
        args.append(g.reshape(1, D_MODEL))
    out_shape, out_specs = [], []
    if write_x:
        out_shape.append(jax.ShapeDtypeStruct((n_rows, D_MODEL), F32))
        out_specs.append(row_out)
    for _ in gains:
        out_shape.append(jax.ShapeDtypeStruct((n_rows, D_MODEL), BF16))
        out_specs.append(row_out)
    return pl.pallas_call(
        functools.partial(_norm_kernel, n_y=n_y, gated=gates is not None, n_g=n_g, write_x=write_x),
        grid=(n_rows // tr,), in_specs=in_specs, out_specs=out_specs, out_shape=out_shape,
        compiler_params=_cparams(("parallel",)), name="norm")(*args)


def _glu_kernel(h_ref, wa_ref, wb_ref, ba_ref, bb_ref, u_ref):
    h = h_ref[...]
    a = _dot(h, wa_ref[0]) + ba_ref[...]
    b = _dot(h, wb_ref[0]) + bb_ref[...]
    u_ref[...] = a * jax.nn.sigmoid(b)


def _glu(h, w, b, layer):
    nj = D_MODEL // TN
    b2 = b.reshape(1, 2 * D_MODEL)
    return pl.pallas_call(
        _glu_kernel, grid=(nj, M_ROWS // TM),
        in_specs=[pl.BlockSpec((TM, D_MODEL), lambda j, i: (i, 0)),
                  pl.BlockSpec((1, D_MODEL, TN), lambda j, i: (layer, 0, j)),
                  pl.BlockSpec((1, D_MODEL, TN), lambda j, i: (layer, 0, j + nj)),
                  pl.BlockSpec((1, TN), lambda j, i: (0, j)),
                  pl.BlockSpec((1, TN), lambda j, i: (0, j + nj))],
        out_specs=pl.BlockSpec((TM, TN), lambda j, i: (i, j)),
        out_shape=jax.ShapeDtypeStruct((M_ROWS, D_MODEL), F32),
        compiler_params=_cparams(("parallel", "parallel")), name="pw1_glu")(h, w, w, b2, b2)


def _top2(h, wr_ref, idx_ref, gate_ref, rows):
    h_hi = h.astype(BF16)
    h_lo = (h - h_hi.astype(F32)).astype(BF16)
    logits = _dot(h_hi, wr_ref[0]) + (_dot(h_hi, wr_ref[1]) + _dot(h_lo, wr_ref[0]))
    lane_i = lax.broadcasted_iota(jnp.int32, logits.shape, 1)
    lane = lane_i.astype(F32)
    logits = jnp.where(lane_i < N_EXPERTS, logits, -jnp.inf)
    m1 = jnp.max(logits, axis=-1, keepdims=True)
    i1 = jnp.min(jnp.where(logits == m1, lane, 128.0), axis=-1, keepdims=True)
    rest = jnp.where(lane == i1, -jnp.inf, logits)
    m2 = jnp.max(rest, axis=-1, keepdims=True)
    i2 = jnp.min(jnp.where(rest == m2, lane, 128.0), axis=-1, keepdims=True)
    e2 = jnp.exp(m2 - m1)
    g1 = 1.0 / (1.0 + e2)
    idx_ref[rows, :] = jnp.where(lane_i == 0, i1, i2).astype(jnp.int32)
    gate_ref[rows, :] = jnp.where(lane_i == 0, g1, e2 * g1)


def _resid_kernel(*refs, routed, n_after):
    n_in = 6 if routed else 5
    refs = refs[:n_in] + refs[n_in + n_after:]
    if routed:
        x_ref, w_ref, b_ref, r_ref, g_ref, wr_ref, xo_ref, h_ref, idx_ref, gate_ref = refs
    else:
        x_ref, w_ref, b_ref, r_ref, g_ref, xo_ref, h_ref = refs
    for s in range(TMR // R_SUB):
        rows = slice(s * R_SUB, (s + 1) * R_SUB)
        xn = r_ref[rows, :] + (_dot(x_ref[rows, :], w_ref[0]) + b_ref[...])
        xo_ref[rows, :] = xn
        h = xn * lax.rsqrt(jnp.mean(xn * xn, axis=-1, keepdims=True) + EPS) * g_ref[...]
        h_ref[rows, :] = h.astype(BF16)
        if routed:
            _top2(h, wr_ref, idx_ref, gate_ref, rows)


def _resid_mm(x, w, b, r, layer, gain, w_router=None, after=()):
    routed = w_router is not None
    row = lambda width: pl.BlockSpec((TMR, width), lambda i: (i, 0))
    vec = pl.BlockSpec((1, D_MODEL), lambda i: (0, 0))
    in_specs = [row(D_MODEL), pl.BlockSpec((1, D_MODEL, D_MODEL), lambda i: (layer, 0, 0)), vec, row(D_MODEL), vec]
    args = [x, w, b.reshape(1, D_MODEL), r, gain.reshape(1, D_MODEL)]
    out_specs = [row(D_MODEL), row(D_MODEL)]
    out_shape = [jax.ShapeDtypeStruct((M_ROWS, D_MODEL), F32), jax.ShapeDtypeStruct((M_ROWS, D_MODEL), BF16)]
    if routed:
        in_specs.append(pl.BlockSpec((2, D_MODEL, 128), lambda i: (0, 0, 0)))
        wr = jnp.zeros((D_MODEL, 128), F32).at[:, :N_EXPERTS].set(w_router)
        wr_hi = wr.astype(BF16)
        args.append(jnp.stack([wr_hi, (wr - wr_hi.astype(F32)).astype(BF16)]))
        out_specs += [row(128), row(128)]
        out_shape += [jax.ShapeDtypeStruct((M_ROWS, 128), jnp.int32), jax.ShapeDtypeStruct((M_ROWS, 128), F32)]
    in_specs += [pl.BlockSpec(memory_space=pl.ANY)] * len(after)
    args += list(after)
    return pl.pallas_call(
        functools.partial(_resid_kernel, routed=routed, n_after=len(after)), grid=(M_ROWS // TMR,),
        in_specs=in_specs, out_specs=out_specs, out_shape=out_shape,
        compiler_params=_cparams(("parallel",)), name="proj_resid_norm")(*args)


def _head_norm_rope(x, gain, cos, sin, bd):
    ssq = _dot((x * x).astype(BF16), bd)
    xn = x * lax.rsqrt(ssq * (1.0 / HEAD_DIM) + EPS) * gain
    lane = lax.broadcasted_iota(jnp.int32, xn.shape, 1)
    first_half = (lane % HEAD_DIM) < (HEAD_DIM // 2)
    rot = jnp.where(first_half, pltpu.roll(xn, SLAB - HEAD_DIM // 2, 1), pltpu.roll(xn, HEAD_DIM // 2, 1))
    return xn * cos + rot * sin


def _q_kernel(h_ref, w_ref, g_ref, cos_ref, sin_ref, bd_ref, q_ref):
    bd, gain = bd_ref[...], g_ref[...]
    scale = HEAD_DIM ** -0.5
    for r in range(TM // Q_SUB):
        rows = slice(r * Q_SUB, (r + 1) * Q_SUB)
        acc = _dot(h_ref[rows, :], w_ref[0])
        cos, sin = cos_ref[rows, :], sin_ref[rows, :]
        for s in range(TN // SLAB):
            q = _head_norm_rope(acc[:, s * SLAB:(s + 1) * SLAB], gain, cos, sin, bd)
            q_ref[rows, s * SLAB:(s + 1) * SLAB] = (q * scale).astype(BF16)


def _queries(h, w, g_q, cos, sin, bd, layer):
    return pl.pallas_call(
        _q_kernel, grid=(D_MODEL // TN, M_ROWS // TM),
        in_specs=[pl.BlockSpec((TM, D_MODEL), lambda j, i: (i, 0)),
                  pl.BlockSpec((1, D_MODEL, TN), lambda j, i: (layer, 0, j)),
                  pl.BlockSpec((1, SLAB), lambda j, i: (0, 0)),
                  pl.BlockSpec((TM, SLAB), lambda j, i: (i, 0)),
                  pl.BlockSpec((TM, SLAB), lambda j, i: (i, 0)),
                  pl.BlockSpec((SLAB, SLAB), lambda j, i: (0, 0))],
        out_specs=pl.BlockSpec((TM, TN), lambda j, i: (i, j)),
        out_shape=jax.ShapeDtypeStruct((M_ROWS, D_MODEL), BF16),
        compiler_params=_cparams(("parallel", "parallel")), name="q_proj")(
            h, w, jnp.tile(g_q, SLAB // HEAD_DIM).reshape(1, SLAB), cos, sin, bd)


def _kv_kernel(h_ref, w_ref, g_ref, cos_ref, sin_ref, bd_ref, k_ref, v_ref):
    acc = _dot(h_ref[...], w_ref[...])
    k_ref[...] = _head_norm_rope(acc[:, :KV_W], g_ref[...], cos_ref[...], sin_ref[...], bd_ref[...])
    v_ref[...] = acc[:, KV_W:]


def _shared_kv(h, w, g_k, cos, sin, bd):
    return pl.pallas_call(
        _kv_kernel, grid=(M_ROWS // TM,),
        in_specs=[pl.BlockSpec((TM, D_MODEL), lambda i: (i, 0)),
                  pl.BlockSpec((D_MODEL, 2 * KV_W), lambda i: (0, 0)),
                  pl.BlockSpec((1, SLAB), lambda i: (0, 0)),
                  pl.BlockSpec((TM, SLAB), lambda i: (i, 0)),
                  pl.BlockSpec((TM, SLAB), lambda i: (i, 0)),
                  pl.BlockSpec((SLAB, SLAB), lambda i: (0, 0))],
        out_specs=[pl.BlockSpec((TM, KV_W), lambda i: (i, 0))] * 2,
        out_shape=[jax.ShapeDtypeStruct((M_ROWS, KV_W), F32)] * 2,
        compiler_params=_cparams(("parallel",)), name="kv_proj")(
            h, w, jnp.tile(g_k, SLAB // HEAD_DIM).reshape(1, SLAB), cos, sin, bd)


def _conv_kernel(*refs, tt, use_prev, has_dst):
    c_ref, ext_ref, acc_ref = refs[-3:]
    ins = refs[:-4] if has_dst else refs[:-3]
    if use_prev:
        u_ref, prev_ref, hist_ref, w_ref, bdw_ref, gln_ref, bln_ref = ins
    else:
        u_ref, hist_ref, w_ref, bdw_ref, gln_ref, bln_ref = ins
    if use_prev:
        first = pl.program_id(1) == 0

        @pl.when(first)
        def _():
            ext_ref[0:HIST, :] = hist_ref[0]

        @pl.when(jnp.logical_not(first))
        def _():
            ext_ref[0:HIST, :] = prev_ref[...]
    else:
        ext_ref[0:HIST, :] = hist_ref[0]
    ext_ref[HIST:HIST + tt, :] = u_ref[...]
    rg = min(CONV_RG, tt)
    off = HIST - (CONV_W - 1)
    for r in range(tt // rg):
        for s in range(D_MODEL // CONV_LS):
            ls = slice(s * CONV_LS, (s + 1) * CONV_LS)
            acc = jnp.zeros((rg, CONV_LS), F32)
            for res in range(8):
                offs = [o for o in range(off, off + CONV_W) if o % 8 == res]
                win = ext_ref[r * rg + offs[0]:r * rg + offs[-1] + rg, ls]
                part = None
                for o in offs:
                    k = o - off
                    term = win[o - offs[0]:o - offs[0] + rg, :] * w_ref[k:k + 1, ls]
                    part = term if part is None else part + term
                acc = acc + part
            acc_ref[r * rg:(r + 1) * rg, ls] = acc
    x = acc_ref[...] + bdw_ref[...]
    xc = x - jnp.mean(x, axis=-1, keepdims=True)
    y = xc * lax.rsqrt(jnp.mean(xc * xc, axis=-1, keepdims=True) + EPS)
    y = y * gln_ref[...] + bln_ref[...]
    c_ref[...] = (y * jax.nn.sigmoid(y)).astype(BF16)


def _conv(u, hist, w_dw, b_dw, g_ln, b_ln, *, row0, n_seq, seq_len, tt, dst=None):
    n_t = seq_len // tt
    use_prev = n_t > 1
    blk0 = row0 // tt
    wpad = jnp.zeros((32, D_MODEL), F32).at[:CONV_W].set(w_dw)
    vec = pl.BlockSpec((1, D_MODEL), lambda s, i: (0, 0))
    in_specs = [pl.BlockSpec((tt, D_MODEL), lambda s, i: (blk0 + s * n_t + i, 0))]
    args = [u]
    if use_prev:
        per = tt // HIST
        in_specs.append(pl.BlockSpec(
            (HIST, D_MODEL), lambda s, i: (jnp.maximum((blk0 + s * n_t + i) * per - 1, 0), 0)))
        args.append(u)
    in_specs += [pl.BlockSpec((1, HIST, D_MODEL), lambda s, i: (s, 0, 0)),
                 pl.BlockSpec((32, D_MODEL), lambda s, i: (0, 0)), vec, vec, vec]
    args += [hist, wpad, b_dw.reshape(1, D_MODEL), g_ln.reshape(1, D_MODEL), b_ln.reshape(1, D_MODEL)]
    aliases = {}
    if dst is not None:
        aliases = {len(args): 0}
        in_specs.append(pl.BlockSpec(memory_space=pl.ANY))
        args.append(dst)
    return pl.pallas_call(
        functools.partial(_conv_kernel, tt=tt, use_prev=use_prev, has_dst=dst is not None), grid=(n_seq, n_t),
        in_specs=in_specs,
        out_specs=pl.BlockSpec((tt, D_MODEL), lambda s, i: (blk0 + s * n_t + i, 0)),
        out_shape=jax.ShapeDtypeStruct((M_ROWS, D_MODEL), BF16),
        input_output_aliases=aliases,
        scratch_shapes=[pltpu.VMEM((HIST + tt, D_MODEL), F32), pltpu.VMEM((tt, D_MODEL), F32)],
        compiler_params=_cparams(("parallel", "arbitrary")), name="conv_ln_swish")(*args)


def _ffn_kernel(be_ref, nv_ref, x_ref, wg_ref, wu_ref, wd_ref, *rest):
    o_ref, acc_ref = rest[-2:]
    i, f = pl.program_id(0), pl.program_id(1)
    n_valid = nv_ref[i]

    @pl.when(jnp.logical_and(f == 0, n_valid <= FFN_SUBS[0]))
    def _():
        acc_ref[...] = jnp.zeros_like(acc_ref)

    def run(subs):
        r0 = 0
        for n in subs:
            rows = slice(r0, r0 + n)
            r0 += n
            x = x_ref[rows, :]
            g = _dot(x, wg_ref[0, 0, 0])
            u = _dot(x, wu_ref[0, 0, 0])
            a = (g * jax.nn.sigmoid(g) * u).astype(BF16)
            acc_ref[rows, :] = jnp.where(f == 0, 0.0, acc_ref[rows, :]) + _dot(a, wd_ref[0, 0])

    @pl.when(n_valid > FFN_SUBS[0])
    def _():
        run(FFN_SUBS)

    @pl.when(jnp.logical_and(n_valid > 0, n_valid <= FFN_SUBS[0]))
    def _():
        run(FFN_SUBS[:1])

    @pl.when(f == pl.num_programs(1) - 1)
    def _():
        o_ref[...] = acc_ref[...].astype(BF16)


def _ffn(x, block_expert, block_rows, wg, wu, wd, layer, *, out_blocks=None, blk0=0, dst=None):
    n_blocks = x.shape[0] // TB
    out_blocks = n_blocks if out_blocks is None else out_blocks
    n_f = wg.shape[2]

    def f_idx(i, f, ok):
        return jnp.where(ok[i] > 0, f, n_f - 1)

    in_specs = [pl.BlockSpec((TB, D_MODEL), lambda i, f, be, ok: (i, 0)),
                pl.BlockSpec((1, 1, 1, D_MODEL, TF),
                             lambda i, f, be, ok: (layer, be[i], f_idx(i, f, ok), 0, 0)),
                pl.BlockSpec((1, 1, 1, D_MODEL, TF),
                             lambda i, f, be, ok: (layer, be[i], f_idx(i, f, ok), 0, 0)),
                pl.BlockSpec((1, 1, TF, D_MODEL), lambda i, f, be, ok: (layer, be[i], f_idx(i, f, ok), 0))]
    args = [block_expert, block_rows, x, wg, wu, wd]
    aliases = {}
    if dst is not None:
        aliases = {len(args): 0}
        in_specs.append(pl.BlockSpec(memory_space=pl.ANY))
        args.append(dst)
    grid_spec = pltpu.PrefetchScalarGridSpec(
        num_scalar_prefetch=2, grid=(n_blocks, n_f), in_specs=in_specs,
        out_specs=pl.BlockSpec((TB, D_MODEL), lambda i, f, be, ok: (blk0 + i, 0)),
        scratch_shapes=[pltpu.VMEM((TB, D_MODEL), F32)])
    return pl.pallas_call(
        _ffn_kernel, grid_spec=grid_spec,
        out_shape=jax.ShapeDtypeStruct((out_blocks * TB, D_MODEL), BF16), input_output_aliases=aliases,
        compiler_params=_cparams(("parallel", "arbitrary")), name="swiglu")(*args)


def _moe(h, idx, wg, wu, wd, layer):
    e_flat = idx[:, :TOP_K].reshape(N_ASSIGN)
    onehot = (e_flat[:, None] == jnp.arange(N_EXPERTS, dtype=jnp.int32)[None, :]).astype(jnp.int32)
    csum = jnp.cumsum(onehot, axis=0)
    rank = jnp.sum(csum * onehot, axis=-1) - 1
    counts = csum[-1]
    padded = (counts + TB - 1) // TB * TB
    pad_ends = jnp.cumsum(padded)
    pad_starts = pad_ends - padded
    dest = pad_starts[e_flat] + rank
    blk_start = jnp.arange(N_BLOCKS, dtype=jnp.int32) * TB
    block_expert = jnp.minimum(jnp.searchsorted(pad_ends, blk_start, side='right'),
                               N_EXPERTS - 1).astype(jnp.int32)
    real_end = pad_starts + counts
    block_rows = jnp.where(blk_start < pad_ends[-1],
                           jnp.clip(real_end[block_expert] - blk_start, 0, TB), 0).astype(jnp.int32)
    src = jnp.zeros((N_BLOCKS * TB,), jnp.int32).at[dest].set(jnp.arange(N_ASSIGN, dtype=jnp.int32) // TOP_K)
    half = N_BLOCKS // 2
    y_buf = _ffn(h[src[:half * TB]], block_expert[:half], block_rows[:half], wg, wu, wd, layer,
                 out_blocks=N_BLOCKS)
    y_buf = _ffn(h[src[half * TB:]], block_expert[half:], block_rows[half:], wg, wu, wd, layer,
                 out_blocks=N_BLOCKS, blk0=half, dst=y_buf)
    d2 = dest.reshape(M_ROWS, TOP_K)
    return y_buf[d2[:, 0]], y_buf[d2[:, 1]]


def _attn_kernel(*refs, tq, qb, hi, prompt):
    q_ref, k_ref, v_ref, km_ref, vm_ref, sink_ref = refs[:6]
    o_ref = refs[-1]
    _attn_body(q_ref, k_ref, v_ref, km_ref, vm_ref, sink_ref, o_ref, tq=tq, qb=qb, hi=hi, prompt=prompt)


def _attn_body(q_ref, k_ref, v_ref, km_ref, vm_ref, sink_ref, o_ref, *, tq, qb, hi, prompt):
    lane = lax.broadcasted_iota(jnp.int32, (KEYS, 128), 1)
    low = lane < HEAD_DIM
    key = lax.broadcasted_iota(jnp.int32, (1, 2 * KEYS), 1) % KEYS
    is_meta = jnp.logical_and(key >= BAND, key < BAND + N_META)
    zpad_k = jnp.zeros((KEYS - BAND - N_META, 128), BF16)
    ones_row = lax.broadcasted_iota(jnp.int32, (2 * KEYS, 128), 0) < KEYS
    ones_low = lax.broadcasted_iota(jnp.int32, (2 * KEYS, 128), 1) < HEAD_DIM
    sum_cols = jnp.where(ones_row == ones_low, 1.0, 0.0).astype(BF16)
    out_low = lax.broadcasted_iota(jnp.int32, (4 * tq, 128), 1) < HEAD_DIM

    def chunk(cc, carry):
        c_glob = pl.program_id(1) * qb + cc
        row0 = pl.multiple_of(cc * tq, tq)
        key0 = pl.multiple_of(c_glob * CHUNK, CHUNK) if prompt else 0
        lo = jnp.maximum(WIN_CHUNKS - c_glob, 0) * CHUNK if prompt else 0
        ok = jnp.logical_or(jnp.logical_and(key >= lo, key < hi), is_meta)
        bias = jnp.where(ok, 0.0, NEG_INF)
        for g in range(N_KV_HEADS):
            gl = slice(g * 128, (g + 1) * 128)
            k_all = jnp.concatenate([k_ref[0, pl.ds(key0, BAND), gl], km_ref[0, :, gl], zpad_k], axis=0)
            v_all = jnp.concatenate([v_ref[0, pl.ds(key0, BAND), gl], vm_ref[0, :, gl], zpad_k], axis=0)
            k2 = jnp.concatenate([jnp.where(low, k_all, 0), jnp.where(low, 0, k_all)], axis=0)
            v2 = jnp.concatenate([jnp.where(low, v_all, 0), jnp.where(low, 0, v_all)], axis=0)
            qp = jnp.concatenate(
                [q_ref[pl.ds(row0, tq), g * 512 + i * 128:g * 512 + (i + 1) * 128] for i in range(4)], axis=0)
            s = lax.dot_general(qp, k2, (((1,), (1,)), ((), ())), preferred_element_type=F32) + bias
            es, sink_e = [], []
            for p in range(2):
                sp = s[:, p * KEYS:(p + 1) * KEYS]
                sk = sink_ref[g, :, p:p + 1]
                m = jnp.maximum(jnp.max(sp, axis=-1, keepdims=True), sk)
                es.append(jnp.exp(sp - m).astype(BF16))
                sink_e.append(jnp.exp(sk - m))
            ox = _dot(jnp.concatenate(es, axis=1), jnp.concatenate([v2, sum_cols], axis=1))
            den = ox[:, 128:] + jnp.where(out_low, sink_e[0], sink_e[1])
            o = ox[:, :128] * (1.0 / den)
            for i in range(4):
                o_ref[pl.ds(row0, tq), g * 512 + i * 128:g * 512 + (i + 1) * 128] = (
                    o[i * tq:(i + 1) * tq, :].astype(BF16))
        return carry

    lax.fori_loop(0, qb, chunk, 0, unroll=min(4, qb))


def _attention(q, kd, vd, kmd, vmd, sinks, *, row0, n_b, n_rows, tq, qb, hi, prompt, dst=None):
    blk_rows = tq * qb
    n_i = n_rows // blk_rows
    blk0 = row0 // blk_rows
    per_b_meta = kmd.shape[0] > 1
    sk = sinks.reshape(N_KV_HEADS, 4, 1, 2)
    sk = jnp.broadcast_to(sk, (N_KV_HEADS, 4, tq, 2)).reshape(N_KV_HEADS, 4 * tq, 2)
    n_keys = kd.shape[1]
    kv_spec = pl.BlockSpec((1, n_keys, 512), lambda b, i: (b, 0, 0))
    meta_spec = pl.BlockSpec((1, N_META, 512), lambda b, i: (b if per_b_meta else 0, 0, 0))
    row_spec = pl.BlockSpec((blk_rows, D_MODEL), lambda b, i: (blk0 + b * n_i + i, 0))
    in_specs = [row_spec, kv_spec, kv_spec, meta_spec, meta_spec,
                pl.BlockSpec((N_KV_HEADS, 4 * tq, 2), lambda b, i: (0, 0, 0))]
    args = [q, kd, vd, kmd, vmd, sk]
    aliases = {}
    if dst is not None:
        aliases = {len(args): 0}
        in_specs.append(pl.BlockSpec(memory_space=pl.ANY))
        args.append(dst)
    return pl.pallas_call(
        functools.partial(_attn_kernel, tq=tq, qb=qb, hi=hi, prompt=prompt),
        grid=(n_b, n_i), in_specs=in_specs, out_specs=row_spec,
        out_shape=jax.ShapeDtypeStruct((M_ROWS, D_MODEL), BF16), input_output_aliases=aliases,
        compiler_params=_cparams(("parallel", "arbitrary")), name="swa")(*args)


def _cast_kernel(w_ref, o_ref):
    o_ref[...] = w_ref[...].astype(BF16)


def _cast_rows(w):
    n_l, n_e, d_hidden, d_out = w.shape
    spec = pl.BlockSpec((1, TF, d_out), lambda e, i: (e, i, 0))
    out = pl.pallas_call(
        _cast_kernel, grid=(n_l * n_e, d_hidden // TF), in_specs=[spec], out_specs=spec,
        out_shape=jax.ShapeDtypeStruct((n_l * n_e, d_hidden, d_out), BF16),
        compiler_params=_cparams(("parallel", "parallel")), name="cast")(w.reshape(n_l * n_e, d_hidden, d_out))
    return out.reshape(n_l, n_e, d_hidden, d_out)


def _tile_cols_kernel(w_ref, o_ref):
    for f in range(o_ref.shape[1]):
        o_ref[0, f] = w_ref[0, :, f * TF:(f + 1) * TF].astype(BF16)


def _tile_cols(w):
    n_l, n_e, d_in, d_hidden = w.shape
    n_f = d_hidden // TF
    out = pl.pallas_call(
        _tile_cols_kernel, grid=(n_l * n_e, d_in // CAST_ROWS),
        in_specs=[pl.BlockSpec((1, CAST_ROWS, d_hidden), lambda e, i: (e, i, 0))],
        out_specs=pl.BlockSpec((1, n_f, CAST_ROWS, TF), lambda e, i: (e, 0, i, 0)),
        out_shape=jax.ShapeDtypeStruct((n_l * n_e, n_f, d_in, TF), BF16),
        compiler_params=_cparams(("parallel", "parallel")), name="cast_tile")(w.reshape(n_l * n_e, d_in, d_hidden))
    return out.reshape(n_l, n_e, n_f, d_in, TF)


def _dup_heads(x):
    lead = x.shape[:-1]
    x = x.astype(BF16).reshape(*lead, N_KV_HEADS, 1, HEAD_DIM)
    return jnp.broadcast_to(x, (*lead, N_KV_HEADS, 2, HEAD_DIM)).reshape(*lead, 2 * KV_W)


def kernel(x_prompt, x_sample, state_conv, cache_meta_k, cache_meta_v, cache_win_k, cache_win_v, meta_tokens, g_conv, w_pw1, b_pw1, w_dw, b_dw, g_conv_ln, b_conv_ln, w_pw2, b_pw2, g_kv, w_kv, g_k, g_attn, w_q, g_q, sinks, w_o, g_ffn, w_ffn_gate, w_ffn_up, w_ffn_down, g_moe, w_router, w_moe_gate, w_moe_up, w_moe_down):
    bf = lambda w: w.astype(BF16)
    x = jnp.concatenate([x_prompt.reshape(N_PROMPT, D_MODEL), x_sample.reshape(N_SAMPLE, D_MODEL),
                         meta_tokens, jnp.zeros((M_ROWS - ROW_META - N_META, D_MODEL), F32)], axis=0)

    pos = jnp.concatenate([jnp.tile(N_META + jnp.arange(SEQ), BATCH),
                           jnp.tile(N_META + PAST_LEN + jnp.arange(DEC_SEQ), DEC_BATCH),
                           jnp.arange(N_META), jnp.zeros((M_ROWS - ROW_META - N_META,), jnp.int32)])
    half = HEAD_DIM // 2
    inv = jnp.exp(jnp.arange(half, dtype=F32) * (-2.0 * math.log(ROPE_THETA) / HEAD_DIM))
    ang = pos.astype(F32)[:, None] * inv[None, :]
    cos_t = jnp.tile(jnp.concatenate([jnp.cos(ang), jnp.cos(ang)], axis=-1), (1, SLAB // HEAD_DIM))
    sin_t = jnp.tile(jnp.concatenate([-jnp.sin(ang), jnp.sin(ang)], axis=-1), (1, SLAB // HEAD_DIM))
    hd = jnp.arange(SLAB) // HEAD_DIM
    bd = (hd[:, None] == hd[None, :]).astype(BF16)

    dense_expert = jnp.zeros((M_ROWS // TB,), jnp.int32)
    dense_rows = jnp.full((M_ROWS // TB,), TB, jnp.int32)
    conv_p, conv_s = [], []
    w_pw1, w_pw2, w_q, w_o, w_kv = bf(w_pw1), bf(w_pw2), bf(w_q), bf(w_o), bf(w_kv)
    w_ffn_gate, w_ffn_up = _tile_cols(w_ffn_gate[:, None]), _tile_cols(w_ffn_up[:, None])
    w_moe_gate, w_moe_up = _tile_cols(w_moe_gate), _tile_cols(w_moe_up)
    w_ffn_down, w_moe_down = _cast_rows(w_ffn_down[:, None]), _cast_rows(w_moe_down)

    (h,) = _norm(x, gains=(g_conv[0],))
    for l in range(N_A):
        u = _glu(h, w_pw1, b_pw1[l], l)
        u_meta = u[ROW_META:ROW_META + N_META]
        u_samp = u[ROW_SAMPLE:ROW_SAMPLE + N_SAMPLE].reshape(DEC_BATCH, DEC_SEQ, D_MODEL)
        hist_p = jnp.broadcast_to(
            jnp.concatenate([jnp.zeros((HIST - N_META, D_MODEL), F32), u_meta], axis=0)[None],
            (BATCH, HIST, D_MODEL))
        hist_s = jnp.zeros((N_SMALL // DEC_SEQ, HIST, D_MODEL), F32).at[:DEC_BATCH, HIST - (CONV_W - 1):].set(
            state_conv[l])
        conv_args = (w_dw[l], b_dw[l], g_conv_ln[l], b_conv_ln[l])
        c = _conv(u, hist_p, *conv_args, row0=0, n_seq=BATCH, seq_len=SEQ, tt=CONV_TT)
        c = _conv(u, hist_s, *conv_args, row0=ROW_SAMPLE, n_seq=N_SMALL // DEC_SEQ, seq_len=DEC_SEQ,
                  tt=DEC_SEQ, dst=c)
        conv_p.append(jnp.stack([u[(b + 1) * SEQ - (CONV_W - 1):(b + 1) * SEQ] for b in range(BATCH)]))
        conv_s.append(jnp.concatenate([state_conv[l], u_samp], axis=1)[:, -(CONV_W - 1):])
        if l == 0:
            x, h = _resid_mm(c, w_pw2, b_pw2[l], x, l, g_ffn[0])
            y = _ffn(h, dense_expert, dense_rows, w_ffn_gate, w_ffn_up, w_ffn_down, 0)
            x, h = _norm(x, ys=(y,), gains=(g_conv[1],))
        else:
            x, h, idx, gates = _resid_mm(c, w_pw2, b_pw2[l], x, l, g_moe[0], w_router[0],
                                         after=(w_moe_gate, w_moe_up, w_moe_down))
            y0, y1 = _moe(h, idx, w_moe_gate, w_moe_up, w_moe_down, 0)
            x, h_kv, h = _norm(x, ys=(y0, y1), gates=gates, gains=(g_kv, g_attn[0]))

    k, v = _shared_kv(h_kv, w_kv, g_k, cos_t, sin_t, bd)
    k_fr = k[:N_PROMPT].reshape(BATCH, SEQ, KV_W)
    v_fr = v[:N_PROMPT].reshape(BATCH, SEQ, KV_W)
    k_new = k[ROW_SAMPLE:ROW_SAMPLE + N_SAMPLE].reshape(DEC_BATCH, DEC_SEQ, KV_W)
    v_new = v[ROW_SAMPLE:ROW_SAMPLE + N_SAMPLE].reshape(DEC_BATCH, DEC_SEQ, KV_W)
    k_meta_p = k[ROW_META:ROW_META + N_META]
    v_meta_p = v[ROW_META:ROW_META + N_META]
    front = jnp.zeros((BATCH, WINDOW, KV_W), F32)
    kd_p = _dup_heads(jnp.concatenate([front, k_fr], axis=1))
    vd_p = _dup_heads(jnp.concatenate([front, v_fr], axis=1))
    tail = jnp.zeros((DEC_BATCH, BAND - WINDOW - DEC_SEQ, KV_W), F32)
    pad_b = lambda t: jnp.concatenate(
        [t, jnp.zeros((N_SMALL // DEC_SEQ - DEC_BATCH, *t.shape[1:]), t.dtype)], axis=0)
    kd_s = _dup_heads(pad_b(jnp.concatenate([cache_win_k.reshape(DEC_BATCH, WINDOW, KV_W), k_new, tail], axis=1)))
    vd_s = _dup_heads(pad_b(jnp.concatenate([cache_win_v.reshape(DEC_BATCH, WINDOW, KV_W), v_new, tail], axis=1)))
    kmd_p, vmd_p = _dup_heads(k_meta_p)[None], _dup_heads(v_meta_p)[None]
    kmd_s = _dup_heads(pad_b(cache_meta_k.reshape(DEC_BATCH, N_META, KV_W)))
    vmd_s = _dup_heads(pad_b(cache_meta_v.reshape(DEC_BATCH, N_META, KV_W)))

    for j in range(DEPTH - N_A):
        q = _queries(h, w_q, g_q[j], cos_t, sin_t, bd, j)
        o = _attention(q, kd_p, vd_p, kmd_p, vmd_p, sinks[j], row0=0, n_b=BATCH, n_rows=SEQ, tq=CHUNK,
                       qb=ATT_QB, hi=BAND, prompt=True)
        o = _attention(q, kd_s, vd_s, kmd_s, vmd_s, sinks[j], row0=ROW_SAMPLE, n_b=N_SMALL // DEC_SEQ,
                       n_rows=DEC_SEQ, tq=DEC_SEQ, qb=1, hi=WINDOW + DEC_SEQ, prompt=False, dst=o)
        no_bias = jnp.zeros((D_MODEL,), F32)
        if j == 0:
            x, h = _resid_mm(o, w_o, no_bias, x, j, g_ffn[1])
            y = _ffn(h, dense_expert, dense_rows, w_ffn_gate, w_ffn_up, w_ffn_down, 1)
            x, h = _norm(x, ys=(y,), gains=(g_attn[1],))
        else:
            x, h, idx, gates = _resid_mm(o, w_o, no_bias, x, j, g_moe[1], w_router[1])
            y0, y1 = _moe(h, idx, w_moe_gate, w_moe_up, w_moe_down, 1)
            (y_prompt,) = _norm(x, ys=(y0, y1), gates=gates, tr=256, row0=0, n_rows=N_PROMPT)
            (y_sample,) = _norm(x, ys=(y0, y1), gates=gates, tr=N_SAMPLE, row0=ROW_SAMPLE, n_rows=N_SAMPLE)

    y_prompt = y_prompt.reshape(BATCH, SEQ, D_MODEL)
    y_sample = y_sample.reshape(DEC_BATCH, DEC_SEQ, D_MODEL)
    kv4 = lambda t: t.reshape(*t.shape[:-1], N_KV_HEADS, HEAD_DIM)
    meta_k_p = jnp.broadcast_to(kv4(k_meta_p)[None], (BATCH, N_META, N_KV_HEADS, HEAD_DIM))
    meta_v_p = jnp.broadcast_to(kv4(v_meta_p)[None], (BATCH, N_META, N_KV_HEADS, HEAD_DIM))
    win_k_s = jnp.concatenate([cache_win_k, kv4(k_new)], axis=1)[:, -WINDOW:]
    win_v_s = jnp.concatenate([cache_win_v, kv4(v_new)], axis=1)[:, -WINDOW:]
    return (y_prompt, y_sample, jnp.stack(conv_p), jnp.stack(conv_s), meta_k_p, meta_v_p,
            kv4(k_fr[:, -WINDOW:]), kv4(v_fr[:, -WINDOW:]), win_k_s, win_v_s)
```
